```python
import math
import jax, jax.numpy as jnp
from jax import lax
import numpy as np

D_MODEL = 2048
BATCH = 4
SEQ = 8192
DEPTH = 1
DEC_BATCH = 4
DEC_SEQ = 2048
PAST_LEN = 128

HEAD_DIM = 128
N_HEADS_A = 8
N_HEADS_B = 8
N_KV_B = 2
D_MIX_A = N_HEADS_A * HEAD_DIM
D_MIX_B = N_HEADS_B * HEAD_DIM
D_MIX = D_MIX_A + D_MIX_B
Q_LORA = 512
KV_LORA = 256
QK_NOPE = 128
QK_ROPE = 64
V_DIM = 128
ROPE_THETA = 10000.0
WINDOW = 128
BLOCK = 128
NUM_BUCKETS = 32
MAX_DISTANCE = 128
N_GROUPS = 4
EXPERTS_PER_GROUP = 8
N_EXPERTS = N_GROUPS * EXPERTS_PER_GROUP
TOP_K = 2
D_FF_EXPERT = 512
MOE_BLOCK = 128
EPS = 1e-6
NEG = -1e30

IN_COLS = (Q_LORA, KV_LORA, QK_ROPE, D_MIX_B, N_KV_B * HEAD_DIM, N_KV_B * HEAD_DIM)
D_IN = Q_LORA + KV_LORA + QK_ROPE + D_MIX_B + 2 * N_KV_B * HEAD_DIM
IN_SPLITS = (Q_LORA, Q_LORA + KV_LORA, Q_LORA + KV_LORA + QK_ROPE,
             Q_LORA + KV_LORA + QK_ROPE + D_MIX_B,
             Q_LORA + KV_LORA + QK_ROPE + D_MIX_B + N_KV_B * HEAD_DIM)

kernel_name = 'hybrid_mla_swa_hmoe_encoder'


def rmsnorm(x, g):
    xf = x.astype(jnp.float32)
    var = jnp.mean(xf * xf, axis=-1, keepdims=True)
    return (xf * lax.rsqrt(var + EPS)).astype(x.dtype) * g


def modulate(h, shift, scale):
    return h * (1 + scale[:, None, :]) + shift[:, None, :]


def rope(x, pos):
    half = QK_ROPE // 2
    inv = ROPE_THETA ** (-jnp.arange(half, dtype=jnp.float32) / half)
    ang = pos.astype(jnp.float32)[:, None] * inv[None, :]
    cos = jnp.cos(ang)[None, :, None, :].astype(x.dtype)
    sin = jnp.sin(ang)[None, :, None, :].astype(x.dtype)
    x1, x2 = x[..., :half], x[..., half:]
    return jnp.concatenate([x1 * cos - x2 * sin, x1 * sin + x2 * cos], axis=-1)


def t5_bucket(rel):
    nb = NUM_BUCKETS // 2
    max_exact = nb // 2
    n = jnp.abs(rel)
    large = max_exact + (jnp.log(jnp.maximum(n, 1).astype(jnp.float32) / max_exact)
                         / math.log(MAX_DISTANCE / max_exact) * (nb - max_exact)).astype(jnp.int32)
    large = jnp.minimum(large, nb - 1)
    return jnp.where(rel > 0, nb, 0) + jnp.where(n < max_exact, n, large)


def mla_mixer(cq, ckv, k_rope, q_norm_g, kv_norm_g, w_uq, w_ukv):
    B, S, _ = cq.shape
    q = jnp.einsum('bsr,rf->bsf', rmsnorm(cq, q_norm_g), w_uq).reshape(B, S, N_HEADS_A, QK_NOPE + QK_ROPE)
    kv = jnp.einsum('bsr,rf->bsf', rmsnorm(ckv, kv_norm_g), w_ukv).reshape(B, S, N_HEADS_A, QK_NOPE + V_DIM)
    pos = jnp.arange(S)
    q_nope = q[..., :QK_NOPE]
    q_rot = rope(q[..., QK_NOPE:], pos)
    k_nope, v = kv[..., :QK_NOPE], kv[..., QK_NOPE:]
    k_rot = rope(k_rope[:, :, None, :], pos)[:, :, 0]
    scale = (QK_NOPE + QK_ROPE) ** -0.5
    nq = S // BLOCK
    qn_blk = q_nope.reshape(B, nq, BLOCK, N_HEADS_A, QK_NOPE).transpose(1, 0, 2, 3, 4)
    qr_blk = q_rot.reshape(B, nq, BLOCK, N_HEADS_A, QK_ROPE).transpose(1, 0, 2, 3, 4)

    def attend(blk):
        qn, qr = blk
        s = jnp.einsum('bqhd,bkhd->bhqk', qn, k_nope) + jnp.einsum('bqhr,bkr->bhqk', qr, k_rot)
        p = jax.nn.softmax(s.astype(jnp.float32) * scale, axis=-1).astype(v.dtype)
        return jnp.einsum('bhqk,bkhd->bqhd', p, v)

    o = lax.map(attend, (qn_blk, qr_blk))
    return o.transpose(1, 0, 2, 3, 4).reshape(B, S, N_HEADS_A * V_DIM)


def swa_mixer(q, k, v, sink, rel_bias):
    B, S, _ = q.shape
    nb = S // BLOCK
    G = N_HEADS_B // N_KV_B
    qb = q.reshape(B, nb, BLOCK, N_KV_B, G, HEAD_DIM)

    def band(t):
        tp = jnp.pad(t.reshape(B, S, N_KV_B, HEAD_DIM), ((0, 0), (BLOCK, BLOCK), (0, 0), (0, 0)))
        tp = tp.reshape(B, nb + 2, BLOCK, N_KV_B, HEAD_DIM)
        return jnp.concatenate([tp[:, :-2], tp[:, 1:-1], tp[:, 2:]], axis=2)

    kb, vb = band(k), band(v)
    koff = jnp.arange(3 * BLOCK) - BLOCK
    rel = koff[None, :] - jnp.arange(BLOCK)[:, None]
    kpos = jnp.arange(nb)[:, None] * BLOCK + koff[None, :]
    valid = (jnp.abs(rel) <= WINDOW)[None] & ((kpos >= 0) & (kpos < S))[:, None, :]
    bias = rel_bias[t5_bucket(rel)].astype(jnp.float32)
    bias = bias.transpose(2, 0, 1).reshape(N_KV_B, G, BLOCK, 3 * BLOCK)
    s = jnp.einsum('bnqkgd,bnckd->bnkgqc', qb, kb).astype(jnp.float32) * HEAD_DIM ** -0.5 + bias
    s = jnp.where(valid[None, :, None, None], s, NEG)
    sink_l = sink.astype(jnp.float32).reshape(N_KV_B, G)[None, None, :, :, None, None]
    lse = jnp.logaddexp(jax.nn.logsumexp(s, axis=-1, keepdims=True), sink_l)
    p = jnp.exp(s - lse).astype(v.dtype)
    o = jnp.einsum('bnkgqc,bnckd->bnqkgd', p, vb)
    return o.reshape(B, S, N_HEADS_B * HEAD_DIM)


def hier_moe(h, router_g, router_g_b, router_e, router_e_b, w1, w3, w2):
    B, S, D = h.shape
    T = B * S
    ht = h.reshape(T, D)
    g_logits = jnp.einsum('td,dg->tg', ht, router_g).astype(jnp.float32) + router_g_b.astype(jnp.float32)
    g_prob = jax.nn.softmax(g_logits, axis=-1)
    g_sel = jnp.argmax(g_logits, axis=-1)
    g_w = jnp.take_along_axis(g_prob, g_sel[:, None], axis=1)
    e_logits = (jnp.einsum('td,de->te', ht, router_e).astype(jnp.float32)
                + router_e_b.astype(jnp.float32)).reshape(T, N_GROUPS, EXPERTS_PER_GROUP)
    e_in = jnp.take_along_axis(e_logits, g_sel[:, None, None], axis=1)[:, 0]
    top_p, top_i = lax.top_k(jax.nn.softmax(e_in, axis=-1), TOP_K)
    gate = g_w * top_p / jnp.sum(top_p, axis=-1, keepdims=True)
    expert = g_sel[:, None] * EXPERTS_PER_GROUP + top_i

    A = T * TOP_K
    e_flat = expert.reshape(A)
    tok_flat = jnp.repeat(jnp.arange(T, dtype=jnp.int32), TOP_K)
    w_flat = gate.reshape(A)
    order = jnp.argsort(e_flat)
    e_s, tok_s, w_s = e_flat[order], tok_flat[order], w_flat[order]
    counts = jnp.zeros((N_EXPERTS,), jnp.int32).at[e_flat].add(1)
    padded = (counts + MOE_BLOCK - 1) // MOE_BLOCK * MOE_BLOCK
    starts = jnp.cumsum(counts) - counts
    pends = jnp.cumsum(padded)
    pstarts = pends - padded
    dest = pstarts[e_s] + jnp.arange(A, dtype=jnp.int32) - starts[e_s]
    n_blk = -(-A // MOE_BLOCK) + N_EXPERTS
    P = n_blk * MOE_BLOCK
    src = jnp.zeros((P,), jnp.int32).at[dest].set(tok_s)
    wbuf = jnp.zeros((P,), jnp.float32).at[dest].set(w_s)
    xbuf = ht[src].reshape(n_blk, MOE_BLOCK, D)
    blk_e = jnp.minimum(jnp.searchsorted(pends, jnp.arange(n_blk) * MOE_BLOCK, side='right'), N_EXPERTS - 1)

    def expert_block(args):
        xb, e = args
        return (jax.nn.silu(xb @ w1[e]) * (xb @ w3[e])) @ w2[e]

    ybuf = lax.map(expert_block, (xbuf, blk_e)).reshape(P, D)
    y = jax.ops.segment_sum(ybuf * wbuf[:, None].astype(h.dtype), src, num_segments=T)
    return y.reshape(B, S, D)


def encoder(x, c, norm1_g, norm2_g, w_ada, b_ada, w_in, q_norm_g, kv_norm_g, w_uq, w_ukv, sink,
            out_norm_a, out_norm_b, w_out, router_g, router_g_b, router_e, router_e_b, w1, w3, w2,
            rel_bias, final_norm_g, w_ada_f, b_ada_f):
    cs = jax.nn.silu(c)
    for l in range(DEPTH):
        mod = cs @ w_ada[l] + b_ada[l]
        sh1, sc1, g1, sh2, sc2, g2 = jnp.split(mod, 6, axis=-1)
        h = modulate(rmsnorm(x, norm1_g[l]), sh1, sc1)
        proj = jnp.einsum('bsd,df->bsf', h, w_in[l])
        cq, ckv, kr, qB, kB, vB = jnp.split(proj, IN_SPLITS, axis=-1)
        oa = mla_mixer(cq, ckv, kr, q_norm_g[l], kv_norm_g[l], w_uq[l], w_ukv[l])
        ob = swa_mixer(qB, kB, vB, sink[l], rel_bias)
        o = jnp.concatenate([rmsnorm(oa, out_norm_a[l]), rmsnorm(ob, out_norm_b[l])], axis=-1)
        x = x + g1[:, None, :] * jnp.einsum('bsf,fd->bsd', o, w_out[l])
        h = modulate(rmsnorm(x, norm2_g[l]), sh2, sc2)
        x = x + g2[:, None, :] * hier_moe(h, router_g[l], router_g_b[l], router_e[l], router_e_b[l],
                                          w1[l], w3[l], w2[l])
    shf, scf = jnp.split(cs @ w_ada_f + b_ada_f, 2, axis=-1)
    return modulate(rmsnorm(x, final_norm_g), shf, scf)


def setup_inputs(seed: int = 0) -> dict:
    key = jax.random.key(seed)
    ks = jax.random.split(key, 32)
    L, D = DEPTH, D_MODEL

    def nrm(k, shape, fan_in):
        return jax.random.normal(k, shape, jnp.float32) * fan_in ** -0.5

    def gain(k, shape):
        return 1.0 + 0.05 * jax.random.normal(k, shape, jnp.float32)

    return {
        'x_prompt': jax.random.normal(ks[0], (BATCH, SEQ, D), jnp.float32),
        'x_sample': jax.random.normal(ks[1], (DEC_BATCH, DEC_SEQ, D), jnp.float32),
        'c_prompt': jax.random.normal(ks[2], (BATCH, D), jnp.float32),
        'c_sample': jax.random.normal(ks[3], (DEC_BATCH, D), jnp.float32),
        'norm1_g': gain(ks[4], (L, D)),
        'norm2_g': gain(ks[5], (L, D)),
        'w_ada': nrm(ks[6], (L, D, 6 * D), D),
        'b_ada': 0.02 * jax.random.normal(ks[7], (L, 6 * D), jnp.float32),
        'w_in': nrm(ks[8], (L, D, D_IN), D),
        'q_norm_g': gain(ks[9], (L, Q_LORA)),
        'kv_norm_g': gain(ks[10], (L, KV_LORA)),
        'w_uq': nrm(ks[11], (L, Q_LORA, N_HEADS_A * (QK_NOPE + QK_ROPE)), Q_LORA),
        'w_ukv': nrm(ks[12], (L, KV_LORA, N_HEADS_A * (QK_NOPE + V_DIM)), KV_LORA),
        'sink': jax.random.normal(ks[13], (L, N_HEADS_B), jnp.float32),
        'out_norm_a': gain(ks[14], (L, D_MIX_A)),
        'out_norm_b': gain(ks[15], (L, D_MIX_B)),
        'w_out': nrm(ks[16], (L, D_MIX, D), D_MIX),
        'router_g': nrm(ks[17], (L, D, N_GROUPS), D),
        'router_g_b': 0.01 * jax.random.normal(ks[18], (L, N_GROUPS), jnp.float32),
        'router_e': nrm(ks[19], (L, D, N_EXPERTS), D),
        'router_e_b': 0.01 * jax.random.normal(ks[20], (L, N_EXPERTS), jnp.float32),
        'w1': nrm(ks[21], (L, N_EXPERTS, D, D_FF_EXPERT), D),
        'w3': nrm(ks[22], (L, N_EXPERTS, D, D_FF_EXPERT), D),
        'w2': nrm(ks[23], (L, N_EXPERTS, D_FF_EXPERT, D), D_FF_EXPERT),
        'rel_bias': 0.5 * jax.random.normal(ks[24], (NUM_BUCKETS, N_HEADS_B), jnp.float32),
        'final_norm_g': gain(ks[25], (D,)),
        'w_ada_f': nrm(ks[26], (D, 2 * D), D),
        'b_ada_f': 0.02 * jax.random.normal(ks[27], (2 * D,), jnp.float32),
    }


def reference(x_prompt, x_sample, c_prompt, c_sample, norm1_g, norm2_g, w_ada, b_ada, w_in, q_norm_g,
              kv_norm_g, w_uq, w_ukv, sink, out_norm_a, out_norm_b, w_out, router_g, router_g_b,
              router_e, router_e_b, w1, w3, w2, rel_bias, final_norm_g, w_ada_f, b_ada_f):
    weights = (norm1_g, norm2_g, w_ada, b_ada, w_in, q_norm_g, kv_norm_g, w_uq, w_ukv, sink,
               out_norm_a, out_norm_b, w_out, router_g, router_g_b, router_e, router_e_b, w1, w3, w2,
               rel_bias, final_norm_g, w_ada_f, b_ada_f)
    y_prompt = encoder(x_prompt, c_prompt, *weights)
    y_sample = encoder(x_sample, c_sample, *weights)
    return (y_prompt, y_sample)
```

```python
import functools
import math

import jax
import jax.numpy as jnp
from jax import lax
from jax.experimental import pallas as pl
from jax.experimental.pallas import tpu as pltpu

F32 = jnp.float32
BF16 = jnp.bfloat16
I32 = jnp.int32

D_MODEL = 2048
HEAD_DIM = 128
N_HEADS_A = 8
N_HEADS_B = 8
N_KV_B = 2
GQA = N_HEADS_B // N_KV_B
D_MIX_A = N_HEADS_A * HEAD_DIM
D_MIX_B = N_HEADS_B * HEAD_DIM
Q_LORA = 512
KV_LORA = 256
QK_NOPE = 128
QK_ROPE = 64
ROPE_HALF = QK_ROPE // 2
V_DIM = 128
ROPE_THETA = 10000.0
WINDOW = 128
BLOCK = 128
NUM_BUCKETS = 32
MAX_DISTANCE = 128
N_GROUPS = 4
EXPERTS_PER_GROUP = 8
N_EXPERTS = N_GROUPS * EXPERTS_PER_GROUP
TOP_K = 2
D_FF_EXPERT = 512
EPS = 1e-6
NEG = -1e30

LANES = 128
MXU_DIM = 256
VMEM_LIMIT = 56 << 20

KR_OFF = Q_LORA + KV_LORA
KRP_OFF = KR_OFF + LANES
QB_OFF = KRP_OFF + LANES
KB_OFF = QB_OFF + D_MIX_B
VB_OFF = KB_OFF + N_KV_B * HEAD_DIM
D_IN_EXT = VB_OFF + N_KV_B * HEAD_DIM
ROUTE_LANES = LANES
E_LANE0 = N_GROUPS

PRE_TM = 256
MLA_TQ = 256
MLA_TK = 512
POST_TM = 256
MOE_BM = 256
DISP_TB = 512
FIN_TC = 512
ADA_ROWS = 16
ADA_TN = 512


def _params(sem):
    return pltpu.CompilerParams(dimension_semantics=sem, vmem_limit_bytes=VMEM_LIMIT)


def _rms(x, g):
    var = jnp.mean(x * x, axis=-1, keepdims=True)
    return (x * lax.rsqrt(var + EPS)) * g


def _ada_kernel(c_ref, w_ref, b_ref, o_ref):
    c = c_ref[...]
    cs = c * jax.nn.sigmoid(c)
    o_ref[...] = jnp.dot(cs.astype(BF16), w_ref[...].astype(BF16), preferred_element_type=F32) + b_ref[...]


def _ada(c_rows, w, b):
    d, n = w.shape
    return pl.pallas_call(
        _ada_kernel,
        out_shape=jax.ShapeDtypeStruct((ADA_ROWS, n), F32),
        grid=(n // ADA_TN,),
        in_specs=[
            pl.BlockSpec((ADA_ROWS, d), lambda j: (0, 0)),
            pl.BlockSpec((d, ADA_TN), lambda j: (0, j)),
            pl.BlockSpec((1, ADA_TN), lambda j: (0, j)),
        ],
        out_specs=pl.BlockSpec((ADA_ROWS, ADA_TN), lambda j: (0, j)),
        compiler_params=_params(("arbitrary",)),
        name="ada",
    )(c_rows, w, b.reshape(1, n))


def _pre_kernel(x_ref, sh_ref, sc_ref, g1_ref, win_ref, gq_ref, gkv_ref, wqt_ref, wkn_ref, wvt_ref,
                cos_ref, sin_ref, cost_ref, sint_ref,
                qt_ref, kn_ref, kr_ref, vt_ref, qb_ref, kb_ref, vb_ref):
    x = x_ref[...]
    h = _rms(x, g1_ref[...]) * (1.0 + sc_ref[...]) + sh_ref[...]
    proj = jnp.dot(h.astype(BF16), win_ref[...], preferred_element_type=F32)

    qb_ref[...] = (proj[:, QB_OFF:KB_OFF] * (HEAD_DIM ** -0.5)).astype(BF16)
    kb_ref[...] = proj[:, KB_OFF:VB_OFF].astype(BF16)
    vb_ref[...] = proj[:, VB_OFF:D_IN_EXT].astype(BF16)

    kr = proj[:, KR_OFF:KRP_OFF] * cos_ref[...] + proj[:, KRP_OFF:QB_OFF] * sin_ref[...]
    kr_ref[...] = kr.astype(BF16)

    ckv = _rms(proj[:, Q_LORA:KR_OFF], gkv_ref[...]).astype(BF16)
    kn_ref[...] = jnp.dot(ckv, wkn_ref[...], preferred_element_type=F32).astype(BF16)
    vt = lax.dot_general(wvt_ref[...], ckv, (((1,), (1,)), ((), ())), preferred_element_type=F32)
    for hh in range(N_HEADS_A):
        vt_ref[hh] = vt[hh * V_DIM:(hh + 1) * V_DIM, :].astype(BF16)

    cq = _rms(proj[:, 0:Q_LORA], gq_ref[...]).astype(BF16)
    qt = lax.dot_general(wqt_ref[...], cq, (((1,), (1,)), ((), ())), preferred_element_type=F32)
    scale = (QK_NOPE + QK_ROPE) ** -0.5
    cost = cost_ref[...]
    sint = sint_ref[...]
    rot0 = N_HEADS_A * QK_NOPE
    par0 = rot0 + N_HEADS_A * LANES
    for hh in range(N_HEADS_A):
        nope = qt[hh * QK_NOPE:(hh + 1) * QK_NOPE, :]
        rot = qt[rot0 + hh * LANES:rot0 + (hh + 1) * LANES, :]
        par = qt[par0 + hh * LANES:par0 + (hh + 1) * LANES, :]
        qt_ref[hh * MXU_DIM:hh * MXU_DIM + QK_NOPE, :] = (nope * scale).astype(BF16)
        qt_ref[hh * MXU_DIM + QK_NOPE:(hh + 1) * MXU_DIM, :] = ((rot * cost + par * sint) * scale).astype(BF16)


def _pre(x, sh, sc, g1, win, gq, gkv, wqt, wkn, wvt, cos, sin, cost, sint):
    b, s, d = x.shape
    tm = PRE_TM
    const2 = lambda bb, i: (0, 0)
    return pl.pallas_call(
        _pre_kernel,
        out_shape=(
            jax.ShapeDtypeStruct((b, N_HEADS_A * MXU_DIM, s), BF16),
            jax.ShapeDtypeStruct((b, s, D_MIX_A), BF16),
            jax.ShapeDtypeStruct((b, s, LANES), BF16),
            jax.ShapeDtypeStruct((b, N_HEADS_A, V_DIM, s), BF16),
            jax.ShapeDtypeStruct((b, s, D_MIX_B), BF16),
            jax.ShapeDtypeStruct((b, s, N_KV_B * HEAD_DIM), BF16),
            jax.ShapeDtypeStruct((b, s, N_KV_B * HEAD_DIM), BF16),
        ),
        grid=(b, s // tm),
        in_specs=[
            pl.BlockSpec((None, tm, d), lambda bb, i: (bb, i, 0)),
            pl.BlockSpec((None, 1, d), lambda bb, i: (bb, 0, 0)),
            pl.BlockSpec((None, 1, d), lambda bb, i: (bb, 0, 0)),
            pl.BlockSpec((1, d), const2),
            pl.BlockSpec(win.shape, const2),
            pl.BlockSpec((1, Q_LORA), const2),
            pl.BlockSpec((1, KV_LORA), const2),
            pl.BlockSpec(wqt.shape, const2),
            pl.BlockSpec(wkn.shape, const2),
            pl.BlockSpec(wvt.shape, const2),
            pl.BlockSpec((tm, LANES), lambda bb, i: (i, 0)),
            pl.BlockSpec((tm, LANES), lambda bb, i: (i, 0)),
            pl.BlockSpec((LANES, tm), lambda bb, i: (0, i)),
            pl.BlockSpec((LANES, tm), lambda bb, i: (0, i)),
        ],
        out_specs=(
            pl.BlockSpec((None, N_HEADS_A * MXU_DIM, tm), lambda bb, i: (bb, 0, i)),
            pl.BlockSpec((None, tm, D_MIX_A), lambda bb, i: (bb, i, 0)),
            pl.BlockSpec((None, tm, LANES), lambda bb, i: (bb, i, 0)),
            pl.BlockSpec((None, N_HEADS_A, V_DIM, tm), lambda bb, i: (bb, 0, 0, i)),
            pl.BlockSpec((None, tm, D_MIX_B), lambda bb, i: (bb, i, 0)),
            pl.BlockSpec((None, tm, N_KV_B * HEAD_DIM), lambda bb, i: (bb, i, 0)),
            pl.BlockSpec((None, tm, N_KV_B * HEAD_DIM), lambda bb, i: (bb, i, 0)),
        ),
        compiler_params=_params(("arbitrary", "arbitrary")),
        name="pre",
    )(x, sh, sc, g1, win, gq, gkv, wqt, wkn, wvt, cos, sin, cost, sint)


def _mla_kernel(qt_ref, kn_ref, kr_ref, vt_ref, o_ref, m_ref, l_ref, acc_ref, *, tk, nk):
    qt = qt_ref[...]
    m_ref[...] = jnp.full(m_ref.shape, -jnp.inf, F32)
    l_ref[...] = jnp.zeros(l_ref.shape, F32)
    acc_ref[...] = jnp.zeros(acc_ref.shape, F32)

    def body(j, carry):
        ks = pl.multiple_of(j * tk, tk)
        kcat = jnp.concatenate([kn_ref[pl.ds(ks, tk), :], kr_ref[pl.ds(ks, tk), :]], axis=1)
        st = jnp.dot(kcat, qt, preferred_element_type=F32)
        m_prev = m_ref[...]
        m_new = jnp.maximum(m_prev, jnp.max(st, axis=0, keepdims=True))
        alpha = jnp.exp(m_prev - m_new)
        p = jnp.exp(st - m_new)
        l_ref[...] = alpha * l_ref[...] + jnp.sum(p, axis=0, keepdims=True)
        pv = jnp.dot(vt_ref[:, pl.ds(ks, tk)], p.astype(BF16), preferred_element_type=F32)
        acc_ref[...] = alpha * acc_ref[...] + pv
        m_ref[...] = m_new
        return carry

    lax.fori_loop(0, nk, body, 0)
    o_ref[...] = (acc_ref[...] / l_ref[...]).T


def _mla(qt, kn, kr, vt):
    b, _, s = qt.shape
    tq, tk = MLA_TQ, min(MLA_TK, s)
    return pl.pallas_call(
        functools.partial(_mla_kernel, tk=tk, nk=s // tk),
        out_shape=jax.ShapeDtypeStruct((b, s, D_MIX_A), F32),
        grid=(b, N_HEADS_A, s // tq),
        in_specs=[
            pl.BlockSpec((None, MXU_DIM, tq), lambda bb, hh, qi: (bb, hh, qi)),
            pl.BlockSpec((None, s, QK_NOPE), lambda bb, hh, qi: (bb, 0, hh)),
            pl.BlockSpec((None, s, LANES), lambda bb, hh, qi: (bb, 0, 0)),
            pl.BlockSpec((None, None, V_DIM, s), lambda bb, hh, qi: (bb, hh, 0, 0)),
        ],
        out_specs=pl.BlockSpec((None, tq, V_DIM), lambda bb, hh, qi: (bb, qi, hh)),
        scratch_shapes=[
            pltpu.VMEM((1, tq), F32),
            pltpu.VMEM((1, tq), F32),
            pltpu.VMEM((V_DIM, tq), F32),
        ],
        compiler_params=_params(("arbitrary", "arbitrary", "arbitrary")),
        name="mla",
    )(qt, kn, kr, vt)


def _swa_kernel(sink_ref, q_ref, kp_ref, kc_ref, kx_ref, vp_ref, vc_ref, vx_ref, bias_ref, o_ref):
    kvh = pl.program_id(1)
    i = pl.program_id(2)
    nb = pl.num_programs(2)
    kband = jnp.concatenate([kp_ref[...], kc_ref[...], kx_ref[...]], axis=0)
    vband = jnp.concatenate([vp_ref[...], vc_ref[...], vx_ref[...]], axis=0)
    q = q_ref[...]
    q4 = jnp.concatenate([q[:, g * HEAD_DIM:(g + 1) * HEAD_DIM] for g in range(GQA)], axis=0)
    s = lax.dot_general(q4, kband, (((1,), (1,)), ((), ())), preferred_element_type=F32)
    s = s + bias_ref[...].reshape(GQA * BLOCK, 3 * BLOCK)
    row = lax.broadcasted_iota(I32, (GQA * BLOCK, 3 * BLOCK), 0) % BLOCK
    col = lax.broadcasted_iota(I32, (GQA * BLOCK, 3 * BLOCK), 1)
    rel = col - BLOCK - row
    kpos = (i - 1) * BLOCK + col
    valid = jnp.logical_and(jnp.abs(rel) <= WINDOW, jnp.logical_and(kpos >= 0, kpos < nb * BLOCK))
    s = jnp.where(valid, s, NEG)
    g_of_row = lax.broadcasted_iota(I32, (GQA * BLOCK, 1), 0) // BLOCK
    sk = jnp.zeros((GQA * BLOCK, 1), F32)
    for g in range(GQA):
        sk = jnp.where(g_of_row == g, sink_ref[kvh * GQA + g], sk)
    m = jnp.maximum(jnp.max(s, axis=-1, keepdims=True), sk)
    e = jnp.exp(s - m)
    denom = jnp.sum(e, axis=-1, keepdims=True) + jnp.exp(sk - m)
    p = (e / denom).astype(BF16)
    o = jnp.dot(p, vband, preferred_element_type=F32)
    for g in range(GQA):
        o_ref[:, g * HEAD_DIM:(g + 1) * HEAD_DIM] = o[g * BLOCK:(g + 1) * BLOCK, :]


def _swa(sink, qb, kb, vb, bias):
    b, s, _ = qb.shape
    nb = s // BLOCK
    prev = lambda bb, k, i: (bb, jnp.maximum(i - 1, 0), k)
    cur = lambda bb, k, i: (bb, i, k)
    nxt = lambda bb, k, i: (bb, jnp.minimum(i + 1, nb - 1), k)
    kv_spec = lambda im: pl.BlockSpec((None, BLOCK, HEAD_DIM), im)
    return pl.pallas_call(
        _swa_kernel,
        out_shape=jax.ShapeDtypeStruct((b, s, D_MIX_B), F32),
        grid=(b, N_KV_B, nb),
        in_specs=[
            pl.BlockSpec(memory_space=pltpu.SMEM),
            pl.BlockSpec((None, BLOCK, GQA * HEAD_DIM), cur),
            kv_spec(prev), kv_spec(cur), kv_spec(nxt),
            kv_spec(prev), kv_spec(cur), kv_spec(nxt),
            pl.BlockSpec((GQA, BLOCK, 3 * BLOCK), lambda bb, k, i: (k, 0, 0)),
        ],
        out_specs=pl.BlockSpec((None, BLOCK, GQA * HEAD_DIM), cur),
        compiler_params=_params(("arbitrary", "arbitrary", "arbitrary")),
        name="swa",
    )(sink, qb, kb, kb, kb, vb, vb, vb, bias)


def _post_kernel(x_ref, oa_ref, ob_ref, ga_ref, gb_ref, wout_ref, g1m_ref, n2g_ref, sh2_ref, sc2_ref,
                 wr_ref, br_ref, x1_ref, h2_ref, route_ref, cnt_ref, run_ref):
    first = jnp.logical_and(pl.program_id(0) == 0, pl.program_id(1) == 0)

    @pl.when(first)
    def _():
        run_ref[...] = jnp.zeros(run_ref.shape, F32)

    o = jnp.concatenate([_rms(oa_ref[...], ga_ref[...]), _rms(ob_ref[...], gb_ref[...])], axis=1)
    y = jnp.dot(o.astype(BF16), wout_ref[...], preferred_element_type=F32)
    x1 = x_ref[...] + g1m_ref[...] * y
    x1_ref[...] = x1
    h2 = _rms(x1, n2g_ref[...]) * (1.0 + sc2_ref[...]) + sh2_ref[...]
    h2_ref[...] = h2
    logits = jnp.dot(h2.astype(BF16), wr_ref[...], preferred_element_type=F32) + br_ref[...]

    tm = logits.shape[0]
    lane = lax.broadcasted_iota(I32, (tm, ROUTE_LANES), 1).astype(F32)
    big = float(ROUTE_LANES)
    gmask = lane < float(N_GROUPS)
    gl = jnp.where(gmask, logits, -jnp.inf)
    gmax = jnp.max(gl, axis=1, keepdims=True)
    gsel = jnp.min(jnp.where(gl == gmax, lane, big), axis=1, keepdims=True)
    gsum = jnp.sum(jnp.where(gmask, jnp.exp(gl - gmax), 0.0), axis=1, keepdims=True)
    g_w = 1.0 / gsum
    lo = float(E_LANE0) + float(EXPERTS_PER_GROUP) * gsel
    emask = jnp.logical_and(lane >= lo, lane < lo + float(EXPERTS_PER_GROUP))
    el = jnp.where(emask, logits, -jnp.inf)
    emax = jnp.max(el, axis=1, keepdims=True)
    ee = jnp.where(emask, jnp.exp(el - emax), 0.0)
    prob = ee / jnp.sum(ee, axis=1, keepdims=True)
    pm = jnp.where(emask, prob, -1.0)
    p1 = jnp.max(pm, axis=1, keepdims=True)
    i1 = jnp.min(jnp.where(pm == p1, lane, big), axis=1, keepdims=True)
    pm2 = jnp.where(lane == i1, -1.0, pm)
    p2 = jnp.max(pm2, axis=1, keepdims=True)
    i2 = jnp.min(jnp.where(pm2 == p2, lane, big), axis=1, keepdims=True)
    psum = p1 + p2
    gate1 = g_w * p1 / psum
    gate2 = g_w * p2 / psum

    oh1 = lane == i1
    oh2 = lane == i2
    c = jnp.where(jnp.logical_or(oh1, oh2), 1.0, 0.0)
    r_i = lax.broadcasted_iota(I32, (tm, tm), 0)
    c_i = lax.broadcasted_iota(I32, (tm, tm), 1)
    ltri = jnp.where(c_i < r_i, 1.0, 0.0).astype(BF16)
    pos = jnp.dot(ltri, c.astype(BF16), preferred_element_type=F32) + run_ref[...]
    rank1 = jnp.sum(jnp.where(oh1, pos, 0.0), axis=1, keepdims=True)
    rank2 = jnp.sum(jnp.where(oh2, pos, 0.0), axis=1, keepdims=True)
    run_new = run_ref[...] + jnp.sum(c, axis=0, keepdims=True)
    run_ref[...] = run_new
    cnt_ref[...] = run_new

    route = jnp.where(lane == 0.0, i1 - float(E_LANE0), 0.0)
    route = jnp.where(lane == 1.0, i2 - float(E_LANE0), route)
    route = jnp.where(lane == 2.0, rank1, route)
    route = jnp.where(lane == 3.0, rank2, route)
    route = jnp.where(lane == 4.0, gate1, route)
    route = jnp.where(lane == 5.0, gate2, route)
    route_ref[...] = route


def _post(x, oa, ob, ga, gb, wout, g1m, n2g, sh2, sc2, wr, br):
    b, s, d = x.shape
    tm = POST_TM
    nt = s // tm
    const2 = lambda bb, i: (0, 0)
    tok = lambda bb, i: (bb, i, 0)
    flat = lambda bb, i: (bb * nt + i, 0)
    vec = lambda bb, i: (bb, 0, 0)
    return pl.pallas_call(
        _post_kernel,
        out_shape=(
            jax.ShapeDtypeStruct((b, s, d), F32),
            jax.ShapeDtypeStruct((b * s, d), F32),
            jax.ShapeDtypeStruct((b * s, ROUTE_LANES), F32),
            jax.ShapeDtypeStruct((1, ROUTE_LANES), F32),
        ),
        grid=(b, nt),
        in_specs=[
            pl.BlockSpec((None, tm, d), tok),
            pl.BlockSpec((None, tm, D_MIX_A), tok),
            pl.BlockSpec((None, tm, D_MIX_B), tok),
            pl.BlockSpec((1, D_MIX_A), const2),
            pl.BlockSpec((1, D_MIX_B), const2),
            pl.BlockSpec(wout.shape, const2),
            pl.BlockSpec((None, 1, d), vec),
            pl.BlockSpec((1, d), const2),
            pl.BlockSpec((None, 1, d), vec),
            pl.BlockSpec((None, 1, d), vec),
            pl.BlockSpec(wr.shape, const2),
            pl.BlockSpec((1, ROUTE_LANES), const2),
        ],
        out_specs=(
            pl.BlockSpec((None, tm, d), tok),
            pl.BlockSpec((tm, d), flat),
            pl.BlockSpec((tm, ROUTE_LANES), flat),
            pl.BlockSpec((1, ROUTE_LANES), const2),
        ),
        scratch_shapes=[pltpu.VMEM((1, ROUTE_LANES), F32)],
        compiler_params=_params(("arbitrary", "arbitrary")),
        name="post",
    )(x, oa, ob, ga, gb, wout, g1m, n2g, sh2, sc2, wr, br)


def _row_copy(src_ref, dst_ref, src_row, dst_row, sem):
    return pltpu.make_async_copy(src_ref.at[pl.ds(src_row, 1), :], dst_ref.at[pl.ds(dst_row, 1), :], sem)


def _dispatch_kernel(pstart_ref, e_ref, r_ref, h_ref, xz_ref, xbuf_ref, sem, *, tb):
    del xz_ref
    base = pl.program_id(0) * tb

    def issue(t, carry):
        for k in range(TOP_K):
            a = TOP_K * t + k
            dst = pstart_ref[e_ref[a]] + r_ref[a]
            _row_copy(h_ref, xbuf_ref, base + t, dst, sem).start()
        return carry

    lax.fori_loop(0, tb, issue, 0)

    def drain(t, carry):
        _row_copy(h_ref, xbuf_ref, 0, 0, sem).wait()
        return carry

    lax.fori_loop(0, TOP_K * tb, drain, 0)


def _dispatch(pstart, e_flat, r_flat, h2, n_rows):
    t, d = h2.shape
    tb = DISP_TB
    xz = jnp.zeros((n_rows, d), F32)
    grid_spec = pltpu.PrefetchScalarGridSpec(
        num_scalar_prefetch=1,
        grid=(t // tb,),
        in_specs=[
            pl.BlockSpec((TOP_K * tb,), lambda i, ps: (i,), memory_space=pltpu.SMEM),
            pl.BlockSpec((TOP_K * tb,), lambda i, ps: (i,), memory_space=pltpu.SMEM),
            pl.BlockSpec(memory_space=pl.ANY),
            pl.BlockSpec(memory_space=pl.ANY),
        ],
        out_specs=pl.BlockSpec(memory_space=pl.ANY),
        scratch_shapes=[pltpu.SemaphoreType.DMA(())],
    )
    return pl.pallas_call(
        functools.partial(_dispatch_kernel, tb=tb),
        out_shape=jax.ShapeDtypeStruct((n_rows, d), F32),
        grid_spec=grid_spec,
        input_output_aliases={4: 0},
        compiler_params=_params(("arbitrary",)),
        name="dispatch",
    )(pstart, e_flat, r_flat, h2, xz)


def _expert_kernel(blk_e_ref, n_used_ref, x_ref, w1_ref, w3_ref, w2_ref, y_ref):
    del blk_e_ref
    j = pl.program_id(0)

    @pl.when(j < n_used_ref[0])
    def _():
        x = x_ref[...].astype(BF16)
        a = jnp.dot(x, w1_ref[...], preferred_element_type=F32)
        g = jnp.dot(x, w3_ref[...], preferred_element_type=F32)
        hmid = (a * jax.nn.sigmoid(a)) * g
        y_ref[...] = jnp.dot(hmid.astype(BF16), w2_ref[...], preferred_element_type=F32)

    @pl.when(j >= n_used_ref[0])
    def _():
        y_ref[...] = jnp.zeros(y_ref.shape, F32)


def _experts(blk_e, n_used, xbuf, w1, w3, w2):
    p, d = xbuf.shape
    bm = MOE_BM
    grid_spec = pltpu.PrefetchScalarGridSpec(
        num_scalar_prefetch=2,
        grid=(p // bm,),
        in_specs=[
            pl.BlockSpec((bm, d), lambda j, be, nu: (j, 0)),
            pl.BlockSpec((None, d, D_FF_EXPERT), lambda j, be, nu: (be[j], 0, 0)),
            pl.BlockSpec((None, d, D_FF_EXPERT), lambda j, be, nu: (be[j], 0, 0)),
            pl.BlockSpec((None, D_FF_EXPERT, d), lambda j, be, nu: (be[j], 0, 0)),
        ],
        out_specs=pl.BlockSpec((bm, d), lambda j, be, nu: (j, 0)),
    )
    return pl.pallas_call(
        _expert_kernel,
        out_shape=jax.ShapeDtypeStruct((p, d), F32),
        grid_spec=grid_spec,
        compiler_params=_params(("arbitrary",)),
        name="experts",
    )(blk_e, n_used, xbuf, w1, w3, w2)


def _final_kernel(pstart_ref, e_ref, r_ref, en_ref, rn_ref, x1_ref, route_ref, g2_ref, fg_ref, shf_ref, scf_ref,
                  ybuf_ref, o_ref, rows_ref, sem, *, tc, n_steps):
    i = pl.program_id(0)
    slot = i % 2

    def issue(e_src, r_src, dst_slot):
        def body(t, carry):
            for k in range(TOP_K):
                a = TOP_K * t + k
                src = pstart_ref[e_src[a]] + r_src[a]
                pltpu.make_async_copy(ybuf_ref.at[pl.ds(src, 1), :], rows_ref.at[dst_slot, k, pl.ds(t, 1), :],
                                      sem.at[dst_slot]).start()
            return carry
        lax.fori_loop(0, tc, body, 0)

    @pl.when(i == 0)
    def _():
        issue(e_ref, r_ref, 0)

    @pl.when(i + 1 < n_steps)
    def _():
        issue(en_ref, rn_ref, 1 - slot)

    def drain(t, carry):
        pltpu.make_async_copy(ybuf_ref.at[pl.ds(0, 1), :], rows_ref.at[slot, 0, pl.ds(0, 1), :], sem.at[slot]).wait()
        return carry

    lax.fori_loop(0, TOP_K * tc, drain, 0)

    route = route_ref[...]
    moe = route[:, 4:5] * rows_ref[slot, 0] + route[:, 5:6] * rows_ref[slot, 1]
    x2 = x1_ref[...] + g2_ref[...] * moe
    o_ref[...] = _rms(x2, fg_ref[...]) * (1.0 + scf_ref[...]) + shf_ref[...]


def _final(pstart, e_flat, r_flat, x1, route, g2, fg, shf, scf, ybuf):
    b, s, d = x1.shape
    tc = FIN_TC
    nt = s // tc
    n_steps = b * nt
    x1f = x1.reshape(b * s, d)
    nxt = lambda i, ps: (jnp.minimum(i + 1, n_steps - 1),)
    vec = lambda i, ps: (i // nt, 0, 0)
    grid_spec = pltpu.PrefetchScalarGridSpec(
        num_scalar_prefetch=1,
        grid=(n_steps,),
        in_specs=[
            pl.BlockSpec((TOP_K * tc,), lambda i, ps: (i,), memory_space=pltpu.SMEM),
            pl.BlockSpec((TOP_K * tc,), lambda i, ps: (i,), memory_space=pltpu.SMEM),
            pl.BlockSpec((TOP_K * tc,), nxt, memory_space=pltpu.SMEM),
            pl.BlockSpec((TOP_K * tc,), nxt, memory_space=pltpu.SMEM),
            pl.BlockSpec((tc, d), lambda i, ps: (i, 0)),
            pl.BlockSpec((tc, ROUTE_LANES), lambda i, ps: (i, 0)),
            pl.BlockSpec((None, 1, d), vec),
            pl.BlockSpec((1, d), lambda i, ps: (0, 0)),
            pl.BlockSpec((None, 1, d), vec),
            pl.BlockSpec((None, 1, d), vec),
            pl.BlockSpec(memory_space=pl.ANY),
        ],
        out_specs=pl.BlockSpec((tc, d), lambda i, ps: (i, 0)),
        scratch_shapes=[
            pltpu.VMEM((2, TOP_K, tc, d), F32),
            pltpu.SemaphoreType.DMA((2,)),
        ],
    )
    out = pl.pallas_call(
        functools.partial(_final_kernel, tc=tc, n_steps=n_steps),
        out_shape=jax.ShapeDtypeStruct((b * s, d), F32),
        grid_spec=grid_spec,
        compiler_params=_params(("arbitrary",)),
        name="final",
    )(pstart, e_flat, r_flat, e_flat, r_flat, x1f, route, g2, fg, shf, scf, ybuf)
    return out.reshape(b, s, d)


def _t5_buckets():
    nb = NUM_BUCKETS // 2
    max_exact = nb // 2
    koff = jnp.arange(3 * BLOCK) - BLOCK
    rel = koff[None, :] - jnp.arange(BLOCK)[:, None]
    n = jnp.abs(rel)
    large = max_exact + (jnp.log(jnp.maximum(n, 1).astype(F32) / max_exact)
                         / math.log(MAX_DISTANCE / max_exact) * (nb - max_exact)).astype(I32)
    large = jnp.minimum(large, nb - 1)
    return jnp.where(rel > 0, nb, 0) + jnp.where(n < max_exact, n, large)


def _rope_tables(s):
    inv = ROPE_THETA ** (-jnp.arange(ROPE_HALF, dtype=F32) / ROPE_HALF)
    ang = jnp.arange(s).astype(F32)[:, None] * inv[None, :]
    pad = jnp.zeros((s, LANES - QK_ROPE), F32)
    cos = jnp.concatenate([jnp.cos(ang), jnp.cos(ang), pad], axis=1)
    sin = jnp.concatenate([jnp.sin(ang), jnp.sin(ang), pad], axis=1)
    return cos, sin, cos.T, sin.T


def _prep_weights(w_in, w_uq, w_ukv, w_out, router_g, router_g_b, router_e, router_e_b, w1, w3, w2):
    d = w_in.shape[0]
    z64 = jnp.zeros((d, LANES - QK_ROPE), F32)
    kr_w = w_in[:, KR_OFF:KR_OFF + QK_ROPE]
    kr_x1, kr_x2 = kr_w[:, :ROPE_HALF], kr_w[:, ROPE_HALF:]
    win = jnp.concatenate([w_in[:, :KR_OFF], kr_w, z64, -kr_x2, kr_x1, z64, w_in[:, KR_OFF + QK_ROPE:]],
                          axis=1).astype(BF16)
    wq = w_uq.reshape(Q_LORA, N_HEADS_A, QK_NOPE + QK_ROPE)
    nope = wq[:, :, :QK_NOPE].reshape(Q_LORA, N_HEADS_A * QK_NOPE)
    r = wq[:, :, QK_NOPE:]
    zr = jnp.zeros((Q_LORA, N_HEADS_A, LANES - QK_ROPE), F32)
    rot = jnp.concatenate([r, zr], axis=2).reshape(Q_LORA, N_HEADS_A * LANES)
    par = jnp.concatenate([-r[:, :, ROPE_HALF:], r[:, :, :ROPE_HALF], zr], axis=2).reshape(Q_LORA, N_HEADS_A * LANES)
    wqt = jnp.concatenate([nope, rot, par], axis=1).T.astype(BF16)
    wkv = w_ukv.reshape(KV_LORA, N_HEADS_A, QK_NOPE + V_DIM)
    wkn = wkv[:, :, :QK_NOPE].reshape(KV_LORA, N_HEADS_A * QK_NOPE).astype(BF16)
    wvt = wkv[:, :, QK_NOPE:].reshape(KV_LORA, N_HEADS_A * V_DIM).T.astype(BF16)
    pad_r = jnp.zeros((d, ROUTE_LANES - N_GROUPS - N_EXPERTS), F32)
    wr = jnp.concatenate([router_g, router_e, pad_r], axis=1).astype(BF16)
    br = jnp.concatenate([router_g_b, router_e_b, jnp.zeros((ROUTE_LANES - N_GROUPS - N_EXPERTS,), F32)]
                         ).reshape(1, ROUTE_LANES)
    return dict(win=win, wqt=wqt, wkn=wkn, wvt=wvt, wout=w_out.astype(BF16), wr=wr, br=br,
                w1=w1.astype(BF16), w3=w3.astype(BF16), w2=w2.astype(BF16))


def _encoder(x, mod, modf, pw, norm1_g, norm2_g, q_norm_g, kv_norm_g, sink, out_norm_a, out_norm_b,
             bias, final_norm_g):
    b, s, d = x.shape
    t = b * s
    sh1, sc1, g1, sh2, sc2, g2 = [m.reshape(b, 1, d) for m in jnp.split(mod, 6, axis=-1)]
    shf, scf = [m.reshape(b, 1, d) for m in jnp.split(modf, 2, axis=-1)]
    cos, sin, cost, sint = _rope_tables(s)

    qt, kn, kr, vt, qb, kb, vb = _pre(x, sh1, sc1, norm1_g.reshape(1, d), pw["win"], q_norm_g.reshape(1, -1),
                                      kv_norm_g.reshape(1, -1), pw["wqt"], pw["wkn"], pw["wvt"],
                                      cos, sin, cost, sint)
    oa = _mla(qt, kn, kr, vt)
    ob = _swa(sink, qb, kb, vb, bias)
    x1, h2, route, cnt = _post(x, oa, ob, out_norm_a.reshape(1, -1), out_norm_b.reshape(1, -1), pw["wout"],
                               g1, norm2_g.reshape(1, d), sh2, sc2, pw["wr"], pw["br"])

    counts = cnt[0, E_LANE0:E_LANE0 + N_EXPERTS].astype(I32)
    padded = (counts + MOE_BM - 1) // MOE_BM * MOE_BM
    pends = jnp.cumsum(padded)
    pstart = (pends - padded).astype(I32)
    n_blk = (TOP_K * t) // MOE_BM + N_EXPERTS
    blk_e = jnp.minimum(jnp.searchsorted(pends, jnp.arange(n_blk, dtype=I32) * MOE_BM, side="right"),
                        N_EXPERTS - 1).astype(I32)
    n_used = (pends[-1:] // MOE_BM).astype(I32)
    e_flat = route[:, 0:TOP_K].astype(I32).reshape(-1)
    r_flat = route[:, TOP_K:2 * TOP_K].astype(I32).reshape(-1)

    xbuf = _dispatch(pstart, e_flat, r_flat, h2, n_blk * MOE_BM)
    ybuf = _experts(blk_e, n_used, xbuf, pw["w1"], pw["w3"], pw["w2"])
    return _final(pstart, e_flat, r_flat, x1, route, g2, final_norm_g.reshape(1, d), shf, scf, ybuf)


def kernel(x_prompt, x_sample, c_prompt, c_sample, norm1_g, norm2_g, w_ada, b_ada, w_in, q_norm_g, kv_norm_g,
           w_uq, w_ukv, sink, out_norm_a, out_norm_b, w_out, router_g, router_g_b, router_e, router_e_b,
           w1, w3, w2, rel_bias, final_norm_g, w_ada_f, b_ada_f):
    bp, bs = c_prompt.shape[0], c_sample.shape[0]
    d = c_prompt.shape[1]
    c_rows = jnp.concatenate([c_prompt, c_sample, jnp.zeros((ADA_ROWS - bp - bs, d), F32)], axis=0)
    mod = _ada(c_rows, w_ada[0], b_ada[0])
    modf = _ada(c_rows, w_ada_f, b_ada_f)
    pw = _prep_weights(w_in[0], w_uq[0], w_ukv[0], w_out[0], router_g[0], router_g_b[0], router_e[0],
                       router_e_b[0], w1[0], w3[0], w2[0])
    bias = rel_bias[_t5_buckets()].astype(F32).transpose(2, 0, 1)
    args = (pw, norm1_g[0], norm2_g[0], q_norm_g[0], kv_norm_g[0], sink[0], out_norm_a[0], out_norm_b[0],
            bias, final_norm_g)
    y_prompt = _encoder(x_prompt, mod[:bp], modf[:bp], *args)
    y_sample = _encoder(x_sample, mod[bp:bp + bs], modf[bp:bp + bs], *args)
    return (y_prompt, y_sample)
```

```python
import functools
import math

import jax
import jax.numpy as jnp
from jax import lax
from jax.experimental import pallas as pl
from jax.experimental.pallas import tpu as pltpu

F32 = jnp.float32
BF16 = jnp.bfloat16
I32 = jnp.int32

D_MODEL = 2048
HEAD_DIM = 128
N_HEADS_A = 8
N_HEADS_B = 8
N_KV_B = 2
GQA = N_HEADS_B // N_KV_B
D_MIX_A = N_HEADS_A * HEAD_DIM
D_MIX_B = N_HEADS_B * HEAD_DIM
Q_LORA = 512
KV_LORA = 256
QK_NOPE = 128
QK_ROPE = 64
ROPE_HALF = QK_ROPE // 2
V_DIM = 128
ROPE_THETA = 10000.0
WINDOW = 128
BLOCK = 128
NUM_BUCKETS = 32
MAX_DISTANCE = 128
N_GROUPS = 4
EXPERTS_PER_GROUP = 8
N_EXPERTS = N_GROUPS * EXPERTS_PER_GROUP
TOP_K = 2
D_FF_EXPERT = 512
EPS = 1e-6
NEG = -1e30

LANES = 128
MXU_DIM = 256
VMEM_LIMIT = 56 << 20

KR_OFF = Q_LORA + KV_LORA
KRP_OFF = KR_OFF + LANES
QB_OFF = KRP_OFF + LANES
KB_OFF = QB_OFF + D_MIX_B
VB_OFF = KB_OFF + N_KV_B * HEAD_DIM
D_IN_EXT = VB_OFF + N_KV_B * HEAD_DIM
ROUTE_LANES = LANES
E_LANE0 = N_GROUPS

PRE_TM = 256
MLA_TQ = 512
MLA_TK = 1024
POST_TM = 256
SWA_TQ = 512
MOE_BM = 256
DISP_TB = 512
FIN_TC = 512
ADA_ROWS = 16
ADA_TN = 512
ISSUE_UNROLL = 4
DRAIN_UNROLL = 64


def _params(sem):
    return pltpu.CompilerParams(dimension_semantics=sem, vmem_limit_bytes=VMEM_LIMIT)


def _rms(x, g):
    var = jnp.mean(x * x, axis=-1, keepdims=True)
    return (x * lax.rsqrt(var + EPS)) * g


def _ada_kernel(c_ref, w_ref, b_ref, o_ref):
    c = c_ref[...]
    cs = c * jax.nn.sigmoid(c)
    o_ref[...] = jnp.dot(cs.astype(BF16), w_ref[...].astype(BF16), preferred_element_type=F32) + b_ref[...]


def _ada(c_rows, w, b):
    d, n = w.shape
    return pl.pallas_call(
        _ada_kernel,
        out_shape=jax.ShapeDtypeStruct((ADA_ROWS, n), F32),
        grid=(n // ADA_TN,),
        in_specs=[
            pl.BlockSpec((ADA_ROWS, d), lambda j: (0, 0)),
            pl.BlockSpec((d, ADA_TN), lambda j: (0, j)),
            pl.BlockSpec((1, ADA_TN), lambda j: (0, j)),
        ],
        out_specs=pl.BlockSpec((ADA_ROWS, ADA_TN), lambda j: (0, j)),
        compiler_params=_params(("arbitrary",)),
        name="ada",
    )(c_rows, w, b.reshape(1, n))


def _pre_kernel(x_ref, sh_ref, sc_ref, g1_ref, win_ref, gq_ref, gkv_ref, wqt_ref, wkn_ref, wvt_ref,
                cos_ref, sin_ref, cost_ref, sint_ref,
                qt_ref, kn_ref, kr_ref, vt_ref, qb_ref, kb_ref, vb_ref):
    x = x_ref[...]
    h = _rms(x, g1_ref[...]) * (1.0 + sc_ref[...]) + sh_ref[...]
    proj = jnp.dot(h.astype(BF16), win_ref[...], preferred_element_type=F32)

    qb_ref[...] = (proj[:, QB_OFF:KB_OFF] * (HEAD_DIM ** -0.5)).astype(BF16)
    kb_ref[...] = proj[:, KB_OFF:VB_OFF].astype(BF16)
    vb_ref[...] = proj[:, VB_OFF:D_IN_EXT].astype(BF16)

    kr = proj[:, KR_OFF:KRP_OFF] * cos_ref[...] + proj[:, KRP_OFF:QB_OFF] * sin_ref[...]
    kr_ref[...] = kr.astype(BF16)

    ckv = _rms(proj[:, Q_LORA:KR_OFF], gkv_ref[...]).astype(BF16)
    kn_ref[...] = jnp.dot(ckv, wkn_ref[...], preferred_element_type=F32).astype(BF16)
    vt = lax.dot_general(wvt_ref[...], ckv, (((1,), (1,)), ((), ())), preferred_element_type=F32)
    for hh in range(N_HEADS_A):
        vt_ref[hh] = vt[hh * V_DIM:(hh + 1) * V_DIM, :].astype(BF16)

    cq = _rms(proj[:, 0:Q_LORA], gq_ref[...]).astype(BF16)
    qt = lax.dot_general(wqt_ref[...], cq, (((1,), (1,)), ((), ())), preferred_element_type=F32)
    scale = (QK_NOPE + QK_ROPE) ** -0.5 * math.log2(math.e)
    cost = cost_ref[...]
    sint = sint_ref[...]
    rot0 = N_HEADS_A * QK_NOPE
    par0 = rot0 + N_HEADS_A * LANES
    for hh in range(N_HEADS_A):
        nope = qt[hh * QK_NOPE:(hh + 1) * QK_NOPE, :]
        rot = qt[rot0 + hh * LANES:rot0 + (hh + 1) * LANES, :]
        par = qt[par0 + hh * LANES:par0 + (hh + 1) * LANES, :]
        qt_ref[hh * MXU_DIM:hh * MXU_DIM + QK_NOPE, :] = (nope * scale).astype(BF16)
        qt_ref[hh * MXU_DIM + QK_NOPE:(hh + 1) * MXU_DIM, :] = ((rot * cost + par * sint) * scale).astype(BF16)


def _pre(x, sh, sc, g1, win, gq, gkv, wqt, wkn, wvt, cos, sin, cost, sint):
    b, s, d = x.shape
    tm = PRE_TM
    const2 = lambda bb, i: (0, 0)
    return pl.pallas_call(
        _pre_kernel,
        out_shape=(
            jax.ShapeDtypeStruct((b, N_HEADS_A * MXU_DIM, s), BF16),
            jax.ShapeDtypeStruct((b, s, D_MIX_A), BF16),
            jax.ShapeDtypeStruct((b, s, LANES), BF16),
            jax.ShapeDtypeStruct((b, N_HEADS_A, V_DIM, s), BF16),
            jax.ShapeDtypeStruct((b, s, D_MIX_B), BF16),
            jax.ShapeDtypeStruct((b, s, N_KV_B * HEAD_DIM), BF16),
            jax.ShapeDtypeStruct((b, s, N_KV_B * HEAD_DIM), BF16),
        ),
        grid=(b, s // tm),
        in_specs=[
            pl.BlockSpec((None, tm, d), lambda bb, i: (bb, i, 0)),
            pl.BlockSpec((None, 1, d), lambda bb, i: (bb, 0, 0)),
            pl.BlockSpec((None, 1, d), lambda bb, i: (bb, 0, 0)),
            pl.BlockSpec((1, d), const2),
            pl.BlockSpec(win.shape, const2),
            pl.BlockSpec((1, Q_LORA), const2),
            pl.BlockSpec((1, KV_LORA), const2),
            pl.BlockSpec(wqt.shape, const2),
            pl.BlockSpec(wkn.shape, const2),
            pl.BlockSpec(wvt.shape, const2),
            pl.BlockSpec((tm, LANES), lambda bb, i: (i, 0)),
            pl.BlockSpec((tm, LANES), lambda bb, i: (i, 0)),
            pl.BlockSpec((LANES, tm), lambda bb, i: (0, i)),
            pl.BlockSpec((LANES, tm), lambda bb, i: (0, i)),
        ],
        out_specs=(
            pl.BlockSpec((None, N_HEADS_A * MXU_DIM, tm), lambda bb, i: (bb, 0, i)),
            pl.BlockSpec((None, tm, D_MIX_A), lambda bb, i: (bb, i, 0)),
            pl.BlockSpec((None, tm, LANES), lambda bb, i: (bb, i, 0)),
            pl.BlockSpec((None, N_HEADS_A, V_DIM, tm), lambda bb, i: (bb, 0, 0, i)),
            pl.BlockSpec((None, tm, D_MIX_B), lambda bb, i: (bb, i, 0)),
            pl.BlockSpec((None, tm, N_KV_B * HEAD_DIM), lambda bb, i: (bb, i, 0)),
            pl.BlockSpec((None, tm, N_KV_B * HEAD_DIM), lambda bb, i: (bb, i, 0)),
        ),
        compiler_params=_params(("arbitrary", "arbitrary")),
        name="pre",
    )(x, sh, sc, g1, win, gq, gkv, wqt, wkn, wvt, cos, sin, cost, sint)


def _mla_kernel(qt_ref, kn_ref, kr_ref, vt_ref, o_ref, acc_ref, *, tk, nk):
    qt = qt_ref[...]

    def scores(j):
        ks = j * tk
        kcat = jnp.concatenate([kn_ref[ks:ks + tk, :], kr_ref[ks:ks + tk, :]], axis=1)
        return jnp.dot(kcat, qt, preferred_element_type=F32)

    m = l = None
    st_next = scores(0)
    for j in range(nk):
        ks = j * tk
        st = st_next
        if j + 1 < nk:
            st_next = scores(j + 1)
        mj = jnp.max(st, axis=0, keepdims=True)
        if j == 0:
            m_new = mj
            p = jnp.exp2(st - m_new)
            l = jnp.sum(p, axis=0, keepdims=True)
            acc_ref[...] = jnp.dot(vt_ref[:, ks:ks + tk], p.astype(BF16), preferred_element_type=F32)
        else:
            m_new = jnp.maximum(m, mj)
            alpha = jnp.exp2(m - m_new)
            p = jnp.exp2(st - m_new)
            l = alpha * l + jnp.sum(p, axis=0, keepdims=True)
            pv = jnp.dot(vt_ref[:, ks:ks + tk], p.astype(BF16), preferred_element_type=F32)
            acc_ref[...] = alpha * acc_ref[...] + pv
        m = m_new
    o_ref[...] = (acc_ref[...] / l).T


def _mla(qt, kn, kr, vt):
    b, _, s = qt.shape
    tq, tk = MLA_TQ, min(MLA_TK, s)
    return pl.pallas_call(
        functools.partial(_mla_kernel, tk=tk, nk=s // tk),
        out_shape=jax.ShapeDtypeStruct((b, s, D_MIX_A), F32),
        grid=(b, N_HEADS_A, s // tq),
        in_specs=[
            pl.BlockSpec((None, MXU_DIM, tq), lambda bb, hh, qi: (bb, hh, qi)),
            pl.BlockSpec((None, s, QK_NOPE), lambda bb, hh, qi: (bb, 0, hh)),
            pl.BlockSpec((None, s, LANES), lambda bb, hh, qi: (bb, 0, 0)),
            pl.BlockSpec((None, None, V_DIM, s), lambda bb, hh, qi: (bb, hh, 0, 0)),
        ],
        out_specs=pl.BlockSpec((None, tq, V_DIM), lambda bb, hh, qi: (bb, qi, hh)),
        scratch_shapes=[pltpu.VMEM((V_DIM, tq), F32)],
        compiler_params=_params(("arbitrary", "arbitrary", "arbitrary")),
        name="mla",
    )(qt, kn, kr, vt)


def _swa_kernel(sink_ref, q_ref, kp_ref, kc_ref, kx_ref, vp_ref, vc_ref, vx_ref, bias_ref, o_ref):
    kvh = pl.program_id(1)
    i = pl.program_id(2)
    n_keys = pl.num_programs(2) * SWA_TQ
    kall = jnp.concatenate([kp_ref[...], kc_ref[...], kx_ref[...]], axis=0)
    vall = jnp.concatenate([vp_ref[...], vc_ref[...], vx_ref[...]], axis=0)
    bias = bias_ref[...].reshape(GQA * BLOCK, 3 * BLOCK)
    row = lax.broadcasted_iota(I32, (GQA * BLOCK, 3 * BLOCK), 0) % BLOCK
    col = lax.broadcasted_iota(I32, (GQA * BLOCK, 3 * BLOCK), 1)
    in_window = jnp.abs(col - BLOCK - row) <= WINDOW
    g_of_row = lax.broadcasted_iota(I32, (GQA * BLOCK, 1), 0) // BLOCK
    sk = jnp.zeros((GQA * BLOCK, 1), F32)
    for g in range(GQA):
        sk = jnp.where(g_of_row == g, sink_ref[kvh * GQA + g], sk)
    for sb in range(SWA_TQ // BLOCK):
        kband = kall[sb * BLOCK:(sb + 3) * BLOCK, :]
        vband = vall[sb * BLOCK:(sb + 3) * BLOCK, :]
        q = q_ref[sb * BLOCK:(sb + 1) * BLOCK, :]
        q4 = jnp.concatenate([q[:, g * HEAD_DIM:(g + 1) * HEAD_DIM] for g in range(GQA)], axis=0)
        s = lax.dot_general(q4, kband, (((1,), (1,)), ((), ())), preferred_element_type=F32) + bias
        kpos = i * SWA_TQ + (sb - 1) * BLOCK + col
        valid = jnp.logical_and(in_window, jnp.logical_and(kpos >= 0, kpos < n_keys))
        s = jnp.where(valid, s, NEG)
        m = jnp.maximum(jnp.max(s, axis=-1, keepdims=True), sk)
        e = jnp.exp(s - m)
        denom = jnp.sum(e, axis=-1, keepdims=True) + jnp.exp(sk - m)
        p = (e / denom).astype(BF16)
        o = jnp.dot(p, vband, preferred_element_type=F32)
        for g in range(GQA):
            o_ref[sb * BLOCK:(sb + 1) * BLOCK, g * HEAD_DIM:(g + 1) * HEAD_DIM] = o[g * BLOCK:(g + 1) * BLOCK, :]


def _swa(sink, qb, kb, vb, bias):
    b, s, _ = qb.shape
    nb = s // BLOCK
    sub = SWA_TQ // BLOCK
    prev = lambda bb, k, i: (bb, jnp.maximum(i * sub - 1, 0), k)
    cur = lambda bb, k, i: (bb, i, k)
    nxt = lambda bb, k, i: (bb, jnp.minimum((i + 1) * sub, nb - 1), k)
    edge_spec = lambda im: pl.BlockSpec((None, BLOCK, HEAD_DIM), im)
    main_spec = pl.BlockSpec((None, SWA_TQ, HEAD_DIM), cur)
    return pl.pallas_call(
        _swa_kernel,
        out_shape=jax.ShapeDtypeStruct((b, s, D_MIX_B), F32),
        grid=(b, N_KV_B, s // SWA_TQ),
        in_specs=[
            pl.BlockSpec(memory_space=pltpu.SMEM),
            pl.BlockSpec((None, SWA_TQ, GQA * HEAD_DIM), cur),
            edge_spec(prev), main_spec, edge_spec(nxt),
            edge_spec(prev), main_spec, edge_spec(nxt),
            pl.BlockSpec((GQA, BLOCK, 3 * BLOCK), lambda bb, k, i: (k, 0, 0)),
        ],
        out_specs=pl.BlockSpec((None, SWA_TQ, GQA * HEAD_DIM), cur),
        compiler_params=_params(("arbitrary", "arbitrary", "arbitrary")),
        name="swa",
    )(sink, qb, kb, kb, kb, vb, vb, vb, bias)


def _bias_kernel(rb_ref, bucket_ref, o_ref):
    bucket = bucket_ref[...]
    for hh in range(N_HEADS_B):
        acc = jnp.zeros(bucket.shape, F32)
        for bkt in range(NUM_BUCKETS):
            acc = jnp.where(bucket == bkt, rb_ref[bkt, hh], acc)
        o_ref[hh] = acc


def _bias(rel_bias, buckets):
    return pl.pallas_call(
        _bias_kernel,
        out_shape=jax.ShapeDtypeStruct((N_HEADS_B, BLOCK, 3 * BLOCK), F32),
        in_specs=[
            pl.BlockSpec(memory_space=pltpu.SMEM),
            pl.BlockSpec((BLOCK, 3 * BLOCK), lambda: (0, 0)),
        ],
        out_specs=pl.BlockSpec((N_HEADS_B, BLOCK, 3 * BLOCK), lambda: (0, 0, 0)),
        name="bias",
    )(rel_bias, buckets)


def _post_kernel(x_ref, oa_ref, ob_ref, ga_ref, gb_ref, wout_ref, g1m_ref, n2g_ref, sh2_ref, sc2_ref,
                 wr_ref, br_ref, x1_ref, h2_ref, route_ref, cnt_ref, run_ref):
    first = jnp.logical_and(pl.program_id(0) == 0, pl.program_id(1) == 0)

    @pl.when(first)
    def _():
        run_ref[...] = jnp.zeros(run_ref.shape, F32)

    o = jnp.concatenate([_rms(oa_ref[...], ga_ref[...]), _rms(ob_ref[...], gb_ref[...])], axis=1)
    y = jnp.dot(o.astype(BF16), wout_ref[...], preferred_element_type=F32)
    x1 = x_ref[...] + g1m_ref[...] * y
    x1_ref[...] = x1
    h2 = _rms(x1, n2g_ref[...]) * (1.0 + sc2_ref[...]) + sh2_ref[...]
    h2_ref[...] = h2
    logits = jnp.dot(h2.astype(BF16), wr_ref[...], preferred_element_type=F32) + br_ref[...]

    tm = logits.shape[0]
    lane = lax.broadcasted_iota(I32, (tm, ROUTE_LANES), 1).astype(F32)
    big = float(ROUTE_LANES)
    gmask = lane < float(N_GROUPS)
    gl = jnp.where(gmask, logits, -jnp.inf)
    gmax = jnp.max(gl, axis=1, keepdims=True)
    gsel = jnp.min(jnp.where(gl == gmax, lane, big), axis=1, keepdims=True)
    gsum = jnp.sum(jnp.where(gmask, jnp.exp(gl - gmax), 0.0), axis=1, keepdims=True)
    g_w = 1.0 / gsum
    lo = float(E_LANE0) + float(EXPERTS_PER_GROUP) * gsel
    emask = jnp.logical_and(lane >= lo, lane < lo + float(EXPERTS_PER_GROUP))
    el = jnp.where(emask, logits, -jnp.inf)
    emax = jnp.max(el, axis=1, keepdims=True)
    ee = jnp.where(emask, jnp.exp(el - emax), 0.0)
    prob = ee / jnp.sum(ee, axis=1, keepdims=True)
    pm = jnp.where(emask, prob, -1.0)
    p1 = jnp.max(pm, axis=1, keepdims=True)
    i1 = jnp.min(jnp.where(pm == p1, lane, big), axis=1, keepdims=True)
    pm2 = jnp.where(lane == i1, -1.0, pm)
    p2 = jnp.max(pm2, axis=1, keepdims=True)
    i2 = jnp.min(jnp.where(pm2 == p2, lane, big), axis=1, keepdims=True)
    psum = p1 + p2
    gate1 = g_w * p1 / psum
    gate2 = g_w * p2 / psum

    oh1 = lane == i1
    oh2 = lane == i2
    c = jnp.where(jnp.logical_or(oh1, oh2), 1.0, 0.0)
    r_i = lax.broadcasted_iota(I32, (tm, tm), 0)
    c_i = lax.broadcasted_iota(I32, (tm, tm), 1)
    ltri = jnp.where(c_i < r_i, 1.0, 0.0).astype(BF16)
    pos = jnp.dot(ltri, c.astype(BF16), preferred_element_type=F32) + run_ref[...]
    rank1 = jnp.sum(jnp.where(oh1, pos, 0.0), axis=1, keepdims=True)
    rank2 = jnp.sum(jnp.where(oh2, pos, 0.0), axis=1, keepdims=True)
    run_new = run_ref[...] + jnp.sum(c, axis=0, keepdims=True)
    run_ref[...] = run_new
    cnt_ref[...] = run_new

    route = jnp.where(lane == 0.0, i1 - float(E_LANE0), 0.0)
    route = jnp.where(lane == 1.0, i2 - float(E_LANE0), route)
    route = jnp.where(lane == 2.0, rank1, route)
    route = jnp.where(lane == 3.0, rank2, route)
    route = jnp.where(lane == 4.0, gate1, route)
    route = jnp.where(lane == 5.0, gate2, route)
    route_ref[...] = route


def _post(x, oa, ob, ga, gb, wout, g1m, n2g, sh2, sc2, wr, br):
    b, s, d = x.shape
    tm = POST_TM
    nt = s // tm
    const2 = lambda bb, i: (0, 0)
    tok = lambda bb, i: (bb, i, 0)
    flat = lambda bb, i: (bb * nt + i, 0)
    vec = lambda bb, i: (bb, 0, 0)
    return pl.pallas_call(
        _post_kernel,
        out_shape=(
            jax.ShapeDtypeStruct((b, s, d), F32),
            jax.ShapeDtypeStruct((b * s, d), F32),
            jax.ShapeDtypeStruct((b * s, ROUTE_LANES), F32),
            jax.ShapeDtypeStruct((1, ROUTE_LANES), F32),
        ),
        grid=(b, nt),
        in_specs=[
            pl.BlockSpec((None, tm, d), tok),
            pl.BlockSpec((None, tm, D_MIX_A), tok),
            pl.BlockSpec((None, tm, D_MIX_B), tok),
            pl.BlockSpec((1, D_MIX_A), const2),
            pl.BlockSpec((1, D_MIX_B), const2),
            pl.BlockSpec(wout.shape, const2),
            pl.BlockSpec((None, 1, d), vec),
            pl.BlockSpec((1, d), const2),
            pl.BlockSpec((None, 1, d), vec),
            pl.BlockSpec((None, 1, d), vec),
            pl.BlockSpec(wr.shape, const2),
            pl.BlockSpec((1, ROUTE_LANES), const2),
        ],
        out_specs=(
            pl.BlockSpec((None, tm, d), tok),
            pl.BlockSpec((tm, d), flat),
            pl.BlockSpec((tm, ROUTE_LANES), flat),
            pl.BlockSpec((1, ROUTE_LANES), const2),
        ),
        scratch_shapes=[pltpu.VMEM((1, ROUTE_LANES), F32)],
        compiler_params=_params(("arbitrary", "arbitrary")),
        name="post",
    )(x, oa, ob, ga, gb, wout, g1m, n2g, sh2, sc2, wr, br)


def _row_copy(src_ref, dst_ref, src_row, dst_row, sem):
    return pltpu.make_async_copy(src_ref.at[pl.ds(src_row, 1), :], dst_ref.at[pl.ds(dst_row, 1), :], sem)


def _drain_rows(src_ref, dst_ref, sem, n_rows):
    def drain(t, carry):
        for _ in range(DRAIN_UNROLL):
            _row_copy(src_ref, dst_ref, 0, 0, sem).wait()
        return carry

    lax.fori_loop(0, n_rows // DRAIN_UNROLL, drain, 0)


def _dispatch_kernel(pstart_ref, cnt_ref, nu_ref, e_ref, r_ref, h_ref, xbuf_ref, zero_ref, sem, zsem, *, tb, n_blk):
    @pl.when(pl.program_id(0) == 0)
    def _():
        zero_ref[...] = jnp.zeros(zero_ref.shape, F32)

        def blk_copy(j):
            return pltpu.make_async_copy(zero_ref, xbuf_ref.at[pl.ds(j * MOE_BM, MOE_BM), :], zsem)

        def blk_start(j, c):
            blk_copy(j).start()
            return c

        def blk_wait(j, c):
            blk_copy(j).wait()
            return c

        lax.fori_loop(nu_ref[0], n_blk, blk_start, 0)
        lax.fori_loop(nu_ref[0], n_blk, blk_wait, 0)

        def per_expert(e, carry):
            cnt = cnt_ref[e]
            first = pstart_ref[e] + cnt
            n_pad = (-cnt) & (MOE_BM - 1)

            def start(r, c):
                _row_copy(zero_ref, xbuf_ref, 0, first + r, zsem).start()
                return c

            def wait(r, c):
                _row_copy(zero_ref, xbuf_ref, 0, 0, zsem).wait()
                return c

            lax.fori_loop(0, n_pad, start, 0)
            lax.fori_loop(0, n_pad, wait, 0)
            return carry

        lax.fori_loop(0, N_EXPERTS, per_expert, 0)

    def issue(t, carry):
        for k in range(TOP_K):
            a = TOP_K * t + k
            dst = pstart_ref[e_ref[a]] + r_ref[a]
            _row_copy(h_ref, xbuf_ref, t, dst, sem).start()
        return carry

    lax.fori_loop(0, tb, issue, 0, unroll=ISSUE_UNROLL)
    _drain_rows(h_ref, xbuf_ref, sem, TOP_K * tb)


def _dispatch(pstart, counts, n_used, e_flat, r_flat, h2, n_blk):
    t, d = h2.shape
    tb = DISP_TB
    grid_spec = pltpu.PrefetchScalarGridSpec(
        num_scalar_prefetch=3,
        grid=(t // tb,),
        in_specs=[
            pl.BlockSpec((TOP_K * tb,), lambda i, ps, cn, nu: (i,), memory_space=pltpu.SMEM),
            pl.BlockSpec((TOP_K * tb,), lambda i, ps, cn, nu: (i,), memory_space=pltpu.SMEM),
            pl.BlockSpec((tb, d), lambda i, ps, cn, nu: (i, 0)),
        ],
        out_specs=pl.BlockSpec(memory_space=pl.ANY),
        scratch_shapes=[
            pltpu.VMEM((MOE_BM, d), F32),
            pltpu.SemaphoreType.DMA(()),
            pltpu.SemaphoreType.DMA(()),
        ],
    )
    return pl.pallas_call(
        functools.partial(_dispatch_kernel, tb=tb, n_blk=n_blk),
        out_shape=jax.ShapeDtypeStruct((n_blk * MOE_BM, d), F32),
        grid_spec=grid_spec,
        compiler_params=_params(("arbitrary",)),
        name="dispatch",
    )(pstart, counts, n_used, e_flat, r_flat, h2)


def _expert_kernel(blk_e_ref, n_used_ref, x_ref, w1_ref, w3_ref, w2_ref, y_ref):
    del blk_e_ref
    j = pl.program_id(0)

    @pl.when(j < n_used_ref[0])
    def _():
        x = x_ref[...].astype(BF16)
        a = jnp.dot(x, w1_ref[...], preferred_element_type=F32)
        g = jnp.dot(x, w3_ref[...], preferred_element_type=F32)
        hmid = (a * jax.nn.sigmoid(a)) * g
        y_ref[...] = jnp.dot(hmid.astype(BF16), w2_ref[...], preferred_element_type=F32)

    @pl.when(j >= n_used_ref[0])
    def _():
        y_ref[...] = jnp.zeros(y_ref.shape, F32)


def _experts(blk_e, n_used, xbuf, w1, w3, w2):
    p, d = xbuf.shape
    bm = MOE_BM
    grid_spec = pltpu.PrefetchScalarGridSpec(
        num_scalar_prefetch=2,
        grid=(p // bm,),
        in_specs=[
            pl.BlockSpec((bm, d), lambda j, be, nu: (jnp.minimum(j, nu[0] - 1), 0)),
            pl.BlockSpec((None, d, D_FF_EXPERT), lambda j, be, nu: (be[j], 0, 0)),
            pl.BlockSpec((None, d, D_FF_EXPERT), lambda j, be, nu: (be[j], 0, 0)),
            pl.BlockSpec((None, D_FF_EXPERT, d), lambda j, be, nu: (be[j], 0, 0)),
        ],
        out_specs=pl.BlockSpec((bm, d), lambda j, be, nu: (j, 0)),
    )
    return pl.pallas_call(
        _expert_kernel,
        out_shape=jax.ShapeDtypeStruct((p, d), F32),
        grid_spec=grid_spec,
        compiler_params=_params(("arbitrary",)),
        name="experts",
    )(blk_e, n_used, xbuf, w1, w3, w2)


def _final_kernel(pstart_ref, e_ref, r_ref, en_ref, rn_ref, x1_ref, route_ref, g2_ref, fg_ref, shf_ref, scf_ref,
                  ybuf_ref, o_ref, rows_ref, sem, *, tc, n_steps):
    i = pl.program_id(0)
    slot = i % 2

    def issue(e_src, r_src, dst_slot):
        def body(t, carry):
            for k in range(TOP_K):
                a = TOP_K * t + k
                src = pstart_ref[e_src[a]] + r_src[a]
                pltpu.make_async_copy(ybuf_ref.at[pl.ds(src, 1), :], rows_ref.at[dst_slot, k, pl.ds(t, 1), :],
                                      sem.at[dst_slot]).start()
            return carry
        lax.fori_loop(0, tc, body, 0, unroll=ISSUE_UNROLL)

    @pl.when(i == 0)
    def _():
        issue(e_ref, r_ref, 0)

    @pl.when(i + 1 < n_steps)
    def _():
        issue(en_ref, rn_ref, 1 - slot)

    _drain_rows(ybuf_ref, rows_ref.at[slot, 0], sem.at[slot], TOP_K * tc)

    route = route_ref[...]
    moe = route[:, 4:5] * rows_ref[slot, 0] + route[:, 5:6] * rows_ref[slot, 1]
    x2 = x1_ref[...] + g2_ref[...] * moe
    o_ref[...] = _rms(x2, fg_ref[...]) * (1.0 + scf_ref[...]) + shf_ref[...]


def _final(pstart, e_flat, r_flat, x1, route, g2, fg, shf, scf, ybuf):
    b, s, d = x1.shape
    tc = FIN_TC
    nt = s // tc
    n_steps = b * nt
    x1f = x1.reshape(b * s, d)
    nxt = lambda i, ps: (jnp.minimum(i + 1, n_steps - 1),)
    vec = lambda i, ps: (i // nt, 0, 0)
    grid_spec = pltpu.PrefetchScalarGridSpec(
        num_scalar_prefetch=1,
        grid=(n_steps,),
        in_specs=[
            pl.BlockSpec((TOP_K * tc,), lambda i, ps: (i,), memory_space=pltpu.SMEM),
            pl.BlockSpec((TOP_K * tc,), lambda i, ps: (i,), memory_space=pltpu.SMEM),
            pl.BlockSpec((TOP_K * tc,), nxt, memory_space=pltpu.SMEM),
            pl.BlockSpec((TOP_K * tc,), nxt, memory_space=pltpu.SMEM),
            pl.BlockSpec((tc, d), lambda i, ps: (i, 0)),
            pl.BlockSpec((tc, ROUTE_LANES), lambda i, ps: (i, 0)),
            pl.BlockSpec((None, 1, d), vec),
            pl.BlockSpec((1, d), lambda i, ps: (0, 0)),
            pl.BlockSpec((None, 1, d), vec),
            pl.BlockSpec((None, 1, d), vec),
            pl.BlockSpec(memory_space=pl.ANY),
        ],
        out_specs=pl.BlockSpec((tc, d), lambda i, ps: (i, 0)),
        scratch_shapes=[
            pltpu.VMEM((2, TOP_K, tc, d), F32),
            pltpu.SemaphoreType.DMA((2,)),
        ],
    )
    out = pl.pallas_call(
        functools.partial(_final_kernel, tc=tc, n_steps=n_steps),
        out_shape=jax.ShapeDtypeStruct((b * s, d), F32),
        grid_spec=grid_spec,
        compiler_params=_params(("arbitrary",)),
        name="final",
    )(pstart, e_flat, r_flat, e_flat, r_flat, x1f, route, g2, fg, shf, scf, ybuf)
    return out.reshape(b, s, d)


def _t5_buckets():
    nb = NUM_BUCKETS // 2
    max_exact = nb // 2
    koff = jnp.arange(3 * BLOCK) - BLOCK
    rel = koff[None, :] - jnp.arange(BLOCK)[:, None]
    n = jnp.abs(rel)
    large = max_exact + (jnp.log(jnp.maximum(n, 1).astype(F32) / max_exact)
                         / math.log(MAX_DISTANCE / max_exact) * (nb - max_exact)).astype(I32)
    large = jnp.minimum(large, nb - 1)
    return jnp.where(rel > 0, nb, 0) + jnp.where(n < max_exact, n, large)


def _rope_tables(s):
    inv = ROPE_THETA ** (-jnp.arange(ROPE_HALF, dtype=F32) / ROPE_HALF)
    ang = jnp.arange(s).astype(F32)[:, None] * inv[None, :]
    pad = jnp.zeros((s, LANES - QK_ROPE), F32)
    cos = jnp.concatenate([jnp.cos(ang), jnp.cos(ang), pad], axis=1)
    sin = jnp.concatenate([jnp.sin(ang), jnp.sin(ang), pad], axis=1)
    return cos, sin, cos.T, sin.T


def _prep_weights(w_in, w_uq, w_ukv, w_out, router_g, router_g_b, router_e, router_e_b, w1, w3, w2):
    d = w_in.shape[0]
    z64 = jnp.zeros((d, LANES - QK_ROPE), F32)
    kr_w = w_in[:, KR_OFF:KR_OFF + QK_ROPE]
    kr_x1, kr_x2 = kr_w[:, :ROPE_HALF], kr_w[:, ROPE_HALF:]
    win = jnp.concatenate([w_in[:, :KR_OFF], kr_w, z64, -kr_x2, kr_x1, z64, w_in[:, KR_OFF + QK_ROPE:]],
                          axis=1).astype(BF16)
    wq = w_uq.reshape(Q_LORA, N_HEADS_A, QK_NOPE + QK_ROPE)
    nope = wq[:, :, :QK_NOPE].reshape(Q_LORA, N_HEADS_A * QK_NOPE)
    r = wq[:, :, QK_NOPE:]
    zr = jnp.zeros((Q_LORA, N_HEADS_A, LANES - QK_ROPE), F32)
    rot = jnp.concatenate([r, zr], axis=2).reshape(Q_LORA, N_HEADS_A * LANES)
    par = jnp.concatenate([-r[:, :, ROPE_HALF:], r[:, :, :ROPE_HALF], zr], axis=2).reshape(Q_LORA, N_HEADS_A * LANES)
    wqt = jnp.concatenate([nope, rot, par], axis=1).T.astype(BF16)
    wkv = w_ukv.reshape(KV_LORA, N_HEADS_A, QK_NOPE + V_DIM)
    wkn = wkv[:, :, :QK_NOPE].reshape(KV_LORA, N_HEADS_A * QK_NOPE).astype(BF16)
    wvt = wkv[:, :, QK_NOPE:].reshape(KV_LORA, N_HEADS_A * V_DIM).T.astype(BF16)
    pad_r = jnp.zeros((d, ROUTE_LANES - N_GROUPS - N_EXPERTS), F32)
    wr = jnp.concatenate([router_g, router_e, pad_r], axis=1).astype(BF16)
    br = jnp.concatenate([router_g_b, router_e_b, jnp.zeros((ROUTE_LANES - N_GROUPS - N_EXPERTS,), F32)]
                         ).reshape(1, ROUTE_LANES)
    return dict(win=win, wqt=wqt, wkn=wkn, wvt=wvt, wout=w_out.astype(BF16), wr=wr, br=br,
                w1=w1.astype(BF16), w3=w3.astype(BF16), w2=w2.astype(BF16))


def _encoder(x, mod, modf, pw, norm1_g, norm2_g, q_norm_g, kv_norm_g, sink, out_norm_a, out_norm_b,
             bias, final_norm_g):
    b, s, d = x.shape
    t = b * s
    sh1, sc1, g1, sh2, sc2, g2 = [m.reshape(b, 1, d) for m in jnp.split(mod, 6, axis=-1)]
    shf, scf = [m.reshape(b, 1, d) for m in jnp.split(modf, 2, axis=-1)]
    cos, sin, cost, sint = _rope_tables(s)

    qt, kn, kr, vt, qb, kb, vb = _pre(x, sh1, sc1, norm1_g.reshape(1, d), pw["win"], q_norm_g.reshape(1, -1),
                                      kv_norm_g.reshape(1, -1), pw["wqt"], pw["wkn"], pw["wvt"],
                                      cos, sin, cost, sint)
    oa = _mla(qt, kn, kr, vt)
    ob = _swa(sink, qb, kb, vb, bias)
    x1, h2, route, cnt = _post(x, oa, ob, out_norm_a.reshape(1, -1), out_norm_b.reshape(1, -1), pw["wout"],
                               g1, norm2_g.reshape(1, d), sh2, sc2, pw["wr"], pw["br"])

    counts = cnt[0, E_LANE0:E_LANE0 + N_EXPERTS].astype(I32)
    padded = (counts + MOE_BM - 1) // MOE_BM * MOE_BM
    pends = jnp.cumsum(padded)
    pstart = (pends - padded).astype(I32)
    n_blk = (TOP_K * t) // MOE_BM + N_EXPERTS
    blk_first = jnp.arange(n_blk, dtype=I32) * MOE_BM
    blk_e = jnp.minimum(jnp.sum((pends[None, :] <= blk_first[:, None]).astype(I32), axis=1), N_EXPERTS - 1)
    n_used = (pends[-1:] // MOE_BM).astype(I32)
    e_flat = route[:, 0:TOP_K].astype(I32).reshape(-1)
    r_flat = route[:, TOP_K:2 * TOP_K].astype(I32).reshape(-1)

    xbuf = _dispatch(pstart, counts, n_used, e_flat, r_flat, h2, n_blk)
    ybuf = _experts(blk_e, n_used, xbuf, pw["w1"], pw["w3"], pw["w2"])
    return _final(pstart, e_flat, r_flat, x1, route, g2, final_norm_g.reshape(1, d), shf, scf, ybuf)


def kernel(x_prompt, x_sample, c_prompt, c_sample, norm1_g, norm2_g, w_ada, b_ada, w_in, q_norm_g, kv_norm_g,
           w_uq, w_ukv, sink, out_norm_a, out_norm_b, w_out, router_g, router_g_b, router_e, router_e_b,
           w1, w3, w2, rel_bias, final_norm_g, w_ada_f, b_ada_f):
    bp, bs = c_prompt.shape[0], c_sample.shape[0]
    d = c_prompt.shape[1]
    c_rows = jnp.concatenate([c_prompt, c_sample, jnp.zeros((ADA_ROWS - bp - bs, d), F32)], axis=0)
    mod = _ada(c_rows, w_ada[0], b_ada[0])
    modf = _ada(c_rows, w_ada_f, b_ada_f)
    pw = _prep_weights(w_in[0], w_uq[0], w_ukv[0], w_out[0], router_g[0], router_g_b[0], router_e[0],
                       router_e_b[0], w1[0], w3[0], w2[0])
    bias = _bias(rel_bias.astype(F32), _t5_buckets().astype(I32))
    args = (pw, norm1_g[0], norm2_g[0], q_norm_g[0], kv_norm_g[0], sink[0], out_norm_a[0], out_norm_b[0],
            bias, final_norm_g)
    y_prompt = _encoder(x_prompt, mod[:bp], modf[:bp], *args)
    y_sample = _encoder(x_sample, mod[bp:bp + bs], modf[bp:bp + bs], *args)
    return (y_prompt, y_sample)
```

```python
import functools
import math

import jax
import jax.numpy as jnp
from jax import lax
from jax.experimental import pallas as pl
from jax.experimental.pallas import tpu as pltpu

F32 = jnp.float32
BF16 = jnp.bfloat16
I32 = jnp.int32
U32 = jnp.uint32

D_MODEL = 2048
HEAD_DIM = 128
N_HEADS_A = 8
N_HEADS_B = 8
N_KV_B = 2
GQA = N_HEADS_B // N_KV_B
D_MIX_A = N_HEADS_A * HEAD_DIM
D_MIX_B = N_HEADS_B * HEAD_DIM
Q_LORA = 512
KV_LORA = 256
QK_NOPE = 128
QK_ROPE = 64
ROPE_HALF = QK_ROPE // 2
V_DIM = 128
ROPE_THETA = 10000.0
WINDOW = 128
BLOCK = 128
NUM_BUCKETS = 32
MAX_DISTANCE = 128
N_GROUPS = 4
EXPERTS_PER_GROUP = 8
N_EXPERTS = N_GROUPS * EXPERTS_PER_GROUP
TOP_K = 2
D_FF_EXPERT = 512
EPS = 1e-6
NEG = -1e30

LANES = 128
MXU_DIM = 256
VMEM_LIMIT = 56 << 20

KR_OFF = Q_LORA + KV_LORA
KRP_OFF = KR_OFF + LANES
QB_OFF = KRP_OFF + LANES
KB_OFF = QB_OFF + D_MIX_B
VB_OFF = KB_OFF + N_KV_B * HEAD_DIM
D_IN_EXT = VB_OFF + N_KV_B * HEAD_DIM
ROUTE_LANES = LANES
E_LANE0 = N_GROUPS

PRE_TM = 256
MLA_TQ = 512
MLA_TK = 1024
POST_TM = 256
SWA_TQ = 512
MOE_BM = 256
DISP_TB = 512
FIN_TC = 512
ADA_ROWS = 16
ADA_TN = 512
DRAIN_UNROLL = 64


def _params(sem):
    return pltpu.CompilerParams(dimension_semantics=sem, vmem_limit_bytes=VMEM_LIMIT)


def _rms(x, g):
    var = jnp.mean(x * x, axis=-1, keepdims=True)
    return (x * lax.rsqrt(var + EPS)) * g


def _ada_kernel(c_ref, w_ref, b_ref, o_ref):
    c = c_ref[...]
    cs = c * jax.nn.sigmoid(c)
    o_ref[...] = jnp.dot(cs.astype(BF16), w_ref[...].astype(BF16), preferred_element_type=F32) + b_ref[...]


def _ada(c_rows, w, b):
    d, n = w.shape
    return pl.pallas_call(
        _ada_kernel,
        out_shape=jax.ShapeDtypeStruct((ADA_ROWS, n), F32),
        grid=(n // ADA_TN,),
        in_specs=[
            pl.BlockSpec((ADA_ROWS, d), lambda j: (0, 0)),
            pl.BlockSpec((d, ADA_TN), lambda j: (0, j)),
            pl.BlockSpec((1, ADA_TN), lambda j: (0, j)),
        ],
        out_specs=pl.BlockSpec((ADA_ROWS, ADA_TN), lambda j: (0, j)),
        compiler_params=_params(("arbitrary",)),
        name="ada",
    )(c_rows, w, b.reshape(1, n))


def _pre_kernel(x_ref, sh_ref, sc_ref, g1_ref, win_ref, gq_ref, gkv_ref, wqt_ref, wkn_ref, wvt_ref,
                cos_ref, sin_ref, cost_ref, sint_ref,
                qt_ref, kn_ref, kr_ref, vt_ref, qb_ref, kb_ref, vb_ref):
    x = x_ref[...]
    h = _rms(x, g1_ref[...]) * (1.0 + sc_ref[...]) + sh_ref[...]
    proj = jnp.dot(h.astype(BF16), win_ref[...], preferred_element_type=F32)

    qb_ref[...] = (proj[:, QB_OFF:KB_OFF] * (HEAD_DIM ** -0.5)).astype(BF16)
    kb_ref[...] = proj[:, KB_OFF:VB_OFF].astype(BF16)
    vb_ref[...] = proj[:, VB_OFF:D_IN_EXT].astype(BF16)

    kr = proj[:, KR_OFF:KRP_OFF] * cos_ref[...] + proj[:, KRP_OFF:QB_OFF] * sin_ref[...]
    kr_ref[...] = kr.astype(BF16)

    ckv = _rms(proj[:, Q_LORA:KR_OFF], gkv_ref[...]).astype(BF16)
    kn_ref[...] = jnp.dot(ckv, wkn_ref[...], preferred_element_type=F32).astype(BF16)
    vt = lax.dot_general(wvt_ref[...], ckv, (((1,), (1,)), ((), ())), preferred_element_type=F32)
    for hh in range(N_HEADS_A):
        vt_ref[hh] = vt[hh * V_DIM:(hh + 1) * V_DIM, :].astype(BF16)

    cq = _rms(proj[:, 0:Q_LORA], gq_ref[...]).astype(BF16)
    qt = lax.dot_general(wqt_ref[...], cq, (((1,), (1,)), ((), ())), preferred_element_type=F32)
    scale = (QK_NOPE + QK_ROPE) ** -0.5 * math.log2(math.e)
    cost = cost_ref[...]
    sint = sint_ref[...]
    rot0 = N_HEADS_A * QK_NOPE
    par0 = rot0 + N_HEADS_A * LANES
    for hh in range(N_HEADS_A):
        nope = qt[hh * QK_NOPE:(hh + 1) * QK_NOPE, :]
        rot = qt[rot0 + hh * LANES:rot0 + (hh + 1) * LANES, :]
        par = qt[par0 + hh * LANES:par0 + (hh + 1) * LANES, :]
        qt_ref[hh * MXU_DIM:hh * MXU_DIM + QK_NOPE, :] = (nope * scale).astype(BF16)
        qt_ref[hh * MXU_DIM + QK_NOPE:(hh + 1) * MXU_DIM, :] = ((rot * cost + par * sint) * scale).astype(BF16)


def _pre(x, sh, sc, g1, win, gq, gkv, wqt, wkn, wvt, cos, sin, cost, sint):
    b, s, d = x.shape
    tm = PRE_TM
    const2 = lambda bb, i: (0, 0)
    return pl.pallas_call(
        _pre_kernel,
        out_shape=(
            jax.ShapeDtypeStruct((b, N_HEADS_A * MXU_DIM, s), BF16),
            jax.ShapeDtypeStruct((b, s, D_MIX_A), BF16),
            jax.ShapeDtypeStruct((b, s, LANES), BF16),
            jax.ShapeDtypeStruct((b, N_HEADS_A, V_DIM, s), BF16),
            jax.ShapeDtypeStruct((b, s, D_MIX_B), BF16),
            jax.ShapeDtypeStruct((b, s, N_KV_B * HEAD_DIM), BF16),
            jax.ShapeDtypeStruct((b, s, N_KV_B * HEAD_DIM), BF16),
        ),
        grid=(b, s // tm),
        in_specs=[
            pl.BlockSpec((None, tm, d), lambda bb, i: (bb, i, 0)),
            pl.BlockSpec((None, 1, d), lambda bb, i: (bb, 0, 0)),
            pl.BlockSpec((None, 1, d), lambda bb, i: (bb, 0, 0)),
            pl.BlockSpec((1, d), const2),
            pl.BlockSpec(win.shape, const2),
            pl.BlockSpec((1, Q_LORA), const2),
            pl.BlockSpec((1, KV_LORA), const2),
            pl.BlockSpec(wqt.shape, const2),
            pl.BlockSpec(wkn.shape, const2),
            pl.BlockSpec(wvt.shape, const2),
            pl.BlockSpec((tm, LANES), lambda bb, i: (i, 0)),
            pl.BlockSpec((tm, LANES), lambda bb, i: (i, 0)),
            pl.BlockSpec((LANES, tm), lambda bb, i: (0, i)),
            pl.BlockSpec((LANES, tm), lambda bb, i: (0, i)),
        ],
        out_specs=(
            pl.BlockSpec((None, N_HEADS_A * MXU_DIM, tm), lambda bb, i: (bb, 0, i)),
            pl.BlockSpec((None, tm, D_MIX_A), lambda bb, i: (bb, i, 0)),
            pl.BlockSpec((None, tm, LANES), lambda bb, i: (bb, i, 0)),
            pl.BlockSpec((None, N_HEADS_A, V_DIM, tm), lambda bb, i: (bb, 0, 0, i)),
            pl.BlockSpec((None, tm, D_MIX_B), lambda bb, i: (bb, i, 0)),
            pl.BlockSpec((None, tm, N_KV_B * HEAD_DIM), lambda bb, i: (bb, i, 0)),
            pl.BlockSpec((None, tm, N_KV_B * HEAD_DIM), lambda bb, i: (bb, i, 0)),
        ),
        compiler_params=_params(("arbitrary", "arbitrary")),
        name="pre",
    )(x, sh, sc, g1, win, gq, gkv, wqt, wkn, wvt, cos, sin, cost, sint)


def _mla_kernel(qt_ref, kn_ref, kr_ref, vt_ref, o_ref, acc_ref, *, tk, nk):
    qt = qt_ref[...]

    def scores(j):
        ks = j * tk
        kcat = jnp.concatenate([kn_ref[ks:ks + tk, :], kr_ref[ks:ks + tk, :]], axis=1)
        return jnp.dot(kcat, qt, preferred_element_type=F32)

    def weighted_values(ks, pb, alpha):
        pv = jnp.dot(vt_ref[:, ks:ks + tk], pb, preferred_element_type=F32)
        return pv if alpha is None else alpha * acc_ref[...] + pv

    m = l = None
    st_next = scores(0)
    pending = None
    for j in range(nk):
        st = st_next
        if j + 1 < nk:
            st_next = scores(j + 1)
        mj = jnp.max(st, axis=0, keepdims=True)
        if j == 0:
            m_new, alpha = mj, None
            p = jnp.exp2(st - m_new)
            l = jnp.sum(p, axis=0, keepdims=True)
        else:
            m_new = jnp.maximum(m, mj)
            alpha = jnp.exp2(m - m_new)
            p = jnp.exp2(st - m_new)
            l = alpha * l + jnp.sum(p, axis=0, keepdims=True)
        pb = p.astype(BF16)
        if pending is not None:
            acc_ref[...] = weighted_values(*pending)
        pending = (j * tk, pb, alpha)
        m = m_new
    o_ref[...] = (weighted_values(*pending) / l).T


def _mla(qt, kn, kr, vt):
    b, _, s = qt.shape
    tq, tk = MLA_TQ, min(MLA_TK, s)
    return pl.pallas_call(
        functools.partial(_mla_kernel, tk=tk, nk=s // tk),
        out_shape=jax.ShapeDtypeStruct((b, s, D_MIX_A), F32),
        grid=(b, N_HEADS_A, s // tq),
        in_specs=[
            pl.BlockSpec((None, MXU_DIM, tq), lambda bb, hh, qi: (bb, hh, qi)),
            pl.BlockSpec((None, s, QK_NOPE), lambda bb, hh, qi: (bb, 0, hh)),
            pl.BlockSpec((None, s, LANES), lambda bb, hh, qi: (bb, 0, 0)),
            pl.BlockSpec((None, None, V_DIM, s), lambda bb, hh, qi: (bb, hh, 0, 0)),
        ],
        out_specs=pl.BlockSpec((None, tq, V_DIM), lambda bb, hh, qi: (bb, qi, hh)),
        scratch_shapes=[pltpu.VMEM((V_DIM, tq), F32)],
        compiler_params=_params(("arbitrary", "arbitrary", "arbitrary")),
        name="mla",
    )(qt, kn, kr, vt)


def _swa_kernel(sink_ref, q_ref, kp_ref, kc_ref, kx_ref, vp_ref, vc_ref, vx_ref, bias_ref, o_ref):
    kvh = pl.program_id(1)
    i = pl.program_id(2)
    n_keys = pl.num_programs(2) * SWA_TQ
    kall = jnp.concatenate([kp_ref[...], kc_ref[...], kx_ref[...]], axis=0)
    vall = jnp.concatenate([vp_ref[...], vc_ref[...], vx_ref[...]], axis=0)
    bias = bias_ref[...].reshape(GQA * BLOCK, 3 * BLOCK)
    row = lax.broadcasted_iota(I32, (GQA * BLOCK, 3 * BLOCK), 0) % BLOCK
    col = lax.broadcasted_iota(I32, (GQA * BLOCK, 3 * BLOCK), 1)
    in_window = jnp.abs(col - BLOCK - row) <= WINDOW
    g_of_row = lax.broadcasted_iota(I32, (GQA * BLOCK, 1), 0) // BLOCK
    sk = jnp.zeros((GQA * BLOCK, 1), F32)
    for g in range(GQA):
        sk = jnp.where(g_of_row == g, sink_ref[kvh * GQA + g], sk)
    for sb in range(SWA_TQ // BLOCK):
        kband = kall[sb * BLOCK:(sb + 3) * BLOCK, :]
        vband = vall[sb * BLOCK:(sb + 3) * BLOCK, :]
        q = q_ref[sb * BLOCK:(sb + 1) * BLOCK, :]
        q4 = jnp.concatenate([q[:, g * HEAD_DIM:(g + 1) * HEAD_DIM] for g in range(GQA)], axis=0)
        s = lax.dot_general(q4, kband, (((1,), (1,)), ((), ())), preferred_element_type=F32) + bias
        kpos = i * SWA_TQ + (sb - 1) * BLOCK + col
        valid = jnp.logical_and(in_window, jnp.logical_and(kpos >= 0, kpos < n_keys))
        s = jnp.where(valid, s, NEG)
        m = jnp.maximum(jnp.max(s, axis=-1, keepdims=True), sk)
        e = jnp.exp(s - m)
        denom = jnp.sum(e, axis=-1, keepdims=True) + jnp.exp(sk - m)
        p = (e / denom).astype(BF16)
        o = jnp.dot(p, vband, preferred_element_type=F32)
        for g in range(GQA):
            o_ref[sb * BLOCK:(sb + 1) * BLOCK, g * HEAD_DIM:(g + 1) * HEAD_DIM] = o[g * BLOCK:(g + 1) * BLOCK, :]


def _swa(sink, qb, kb, vb, bias):
    b, s, _ = qb.shape
    nb = s // BLOCK
    sub = SWA_TQ // BLOCK
    prev = lambda bb, k, i: (bb, jnp.maximum(i * sub - 1, 0), k)
    cur = lambda bb, k, i: (bb, i, k)
    nxt = lambda bb, k, i: (bb, jnp.minimum((i + 1) * sub, nb - 1), k)
    edge_spec = lambda im: pl.BlockSpec((None, BLOCK, HEAD_DIM), im)
    main_spec = pl.BlockSpec((None, SWA_TQ, HEAD_DIM), cur)
    return pl.pallas_call(
        _swa_kernel,
        out_shape=jax.ShapeDtypeStruct((b, s, D_MIX_B), F32),
        grid=(b, N_KV_B, s // SWA_TQ),
        in_specs=[
            pl.BlockSpec(memory_space=pltpu.SMEM),
            pl.BlockSpec((None, SWA_TQ, GQA * HEAD_DIM), cur),
            edge_spec(prev), main_spec, edge_spec(nxt),
            edge_spec(prev), main_spec, edge_spec(nxt),
            pl.BlockSpec((GQA, BLOCK, 3 * BLOCK), lambda bb, k, i: (k, 0, 0)),
        ],
        out_specs=pl.BlockSpec((None, SWA_TQ, GQA * HEAD_DIM), cur),
        compiler_params=_params(("arbitrary", "arbitrary", "arbitrary")),
        name="swa",
    )(sink, qb, kb, kb, kb, vb, vb, vb, bias)


def _bias_kernel(rb_ref, bucket_ref, o_ref):
    bucket = bucket_ref[...]
    for hh in range(N_HEADS_B):
        acc = jnp.zeros(bucket.shape, F32)
        for bkt in range(NUM_BUCKETS):
            acc = jnp.where(bucket == bkt, rb_ref[bkt, hh], acc)
        o_ref[hh] = acc


def _bias(rel_bias, buckets):
    return pl.pallas_call(
        _bias_kernel,
        out_shape=jax.ShapeDtypeStruct((N_HEADS_B, BLOCK, 3 * BLOCK), F32),
        in_specs=[
            pl.BlockSpec(memory_space=pltpu.SMEM),
            pl.BlockSpec((BLOCK, 3 * BLOCK), lambda: (0, 0)),
        ],
        out_specs=pl.BlockSpec((N_HEADS_B, BLOCK, 3 * BLOCK), lambda: (0, 0, 0)),
        name="bias",
    )(rel_bias, buckets)


def _post_kernel(x_ref, oa_ref, ob_ref, ga_ref, gb_ref, wout_ref, g1m_ref, n2g_ref, sh2_ref, sc2_ref,
                 wr_ref, br_ref, x1_ref, h2_ref, route_ref, cnt_ref, run_ref):
    first = jnp.logical_and(pl.program_id(0) == 0, pl.program_id(1) == 0)

    @pl.when(first)
    def _():
        run_ref[...] = jnp.zeros(run_ref.shape, F32)

    o = jnp.concatenate([_rms(oa_ref[...], ga_ref[...]), _rms(ob_ref[...], gb_ref[...])], axis=1)
    y = jnp.dot(o.astype(BF16), wout_ref[...], preferred_element_type=F32)
    x1 = x_ref[...] + g1m_ref[...] * y
    x1_ref[...] = x1
    h2 = _rms(x1, n2g_ref[...]) * (1.0 + sc2_ref[...]) + sh2_ref[...]
    h2_ref[...] = _pack_bf16_pairs(h2)
    logits = jnp.dot(h2.astype(BF16), wr_ref[...], preferred_element_type=F32) + br_ref[...]

    tm = logits.shape[0]
    lane = lax.broadcasted_iota(I32, (tm, ROUTE_LANES), 1).astype(F32)
    big = float(ROUTE_LANES)
    gmask = lane < float(N_GROUPS)
    gl = jnp.where(gmask, logits, -jnp.inf)
    gmax = jnp.max(gl, axis=1, keepdims=True)
    gsel = jnp.min(jnp.where(gl == gmax, lane, big), axis=1, keepdims=True)
    gsum = jnp.sum(jnp.where(gmask, jnp.exp(gl - gmax), 0.0), axis=1, keepdims=True)
    g_w = 1.0 / gsum
    lo = float(E_LANE0) + float(EXPERTS_PER_GROUP) * gsel
    emask = jnp.logical_and(lane >= lo, lane < lo + float(EXPERTS_PER_GROUP))
    el = jnp.where(emask, logits, -jnp.inf)
    emax = jnp.max(el, axis=1, keepdims=True)
    ee = jnp.where(emask, jnp.exp(el - emax), 0.0)
    prob = ee / jnp.sum(ee, axis=1, keepdims=True)
    pm = jnp.where(emask, prob, -1.0)
    p1 = jnp.max(pm, axis=1, keepdims=True)
    i1 = jnp.min(jnp.where(pm == p1, lane, big), axis=1, keepdims=True)
    pm2 = jnp.where(lane == i1, -1.0, pm)
    p2 = jnp.max(pm2, axis=1, keepdims=True)
    i2 = jnp.min(jnp.where(pm2 == p2, lane, big), axis=1, keepdims=True)
    psum = p1 + p2
    gate1 = g_w * p1 / psum
    gate2 = g_w * p2 / psum

    oh1 = lane == i1
    oh2 = lane == i2
    c = jnp.where(jnp.logical_or(oh1, oh2), 1.0, 0.0)
    r_i = lax.broadcasted_iota(I32, (tm, tm), 0)
    c_i = lax.broadcasted_iota(I32, (tm, tm), 1)
    ltri = jnp.where(c_i < r_i, 1.0, 0.0).astype(BF16)
    pos = jnp.dot(ltri, c.astype(BF16), preferred_element_type=F32) + run_ref[...]
    rank1 = jnp.sum(jnp.where(oh1, pos, 0.0), axis=1, keepdims=True)
    rank2 = jnp.sum(jnp.where(oh2, pos, 0.0), axis=1, keepdims=True)
    run_new = run_ref[...] + jnp.sum(c, axis=0, keepdims=True)
    run_ref[...] = run_new
    cnt_ref[...] = run_new

    route = jnp.where(lane == 0.0, i1 - float(E_LANE0), 0.0)
    route = jnp.where(lane == 1.0, i2 - float(E_LANE0), route)
    route = jnp.where(lane == 2.0, rank1, route)
    route = jnp.where(lane == 3.0, rank2, route)
    route = jnp.where(lane == 4.0, gate1, route)
    route = jnp.where(lane == 5.0, gate2, route)
    route_ref[...] = route


def _post(x, oa, ob, ga, gb, wout, g1m, n2g, sh2, sc2, wr, br):
    b, s, d = x.shape
    tm = POST_TM
    nt = s // tm
    const2 = lambda bb, i: (0, 0)
    tok = lambda bb, i: (bb, i, 0)
    flat = lambda bb, i: (bb * nt + i, 0)
    vec = lambda bb, i: (bb, 0, 0)
    return pl.pallas_call(
        _post_kernel,
        out_shape=(
            jax.ShapeDtypeStruct((b, s, d), F32),
            jax.ShapeDtypeStruct((b * s, d // 2), U32),
            jax.ShapeDtypeStruct((b * s, ROUTE_LANES), F32),
            jax.ShapeDtypeStruct((1, ROUTE_LANES), F32),
        ),
        grid=(b, nt),
        in_specs=[
            pl.BlockSpec((None, tm, d), tok),
            pl.BlockSpec((None, tm, D_MIX_A), tok),
            pl.BlockSpec((None, tm, D_MIX_B), tok),
            pl.BlockSpec((1, D_MIX_A), const2),
            pl.BlockSpec((1, D_MIX_B), const2),
            pl.BlockSpec(wout.shape, const2),
            pl.BlockSpec((None, 1, d), vec),
            pl.BlockSpec((1, d), const2),
            pl.BlockSpec((None, 1, d), vec),
            pl.BlockSpec((None, 1, d), vec),
            pl.BlockSpec(wr.shape, const2),
            pl.BlockSpec((1, ROUTE_LANES), const2),
        ],
        out_specs=(
            pl.BlockSpec((None, tm, d), tok),
            pl.BlockSpec((tm, d // 2), flat),
            pl.BlockSpec((tm, ROUTE_LANES), flat),
            pl.BlockSpec((1, ROUTE_LANES), const2),
        ),
        scratch_shapes=[pltpu.VMEM((1, ROUTE_LANES), F32)],
        compiler_params=_params(("arbitrary", "arbitrary")),
        name="post",
    )(x, oa, ob, ga, gb, wout, g1m, n2g, sh2, sc2, wr, br)


def _pack_bf16_pairs(x):
    half = x.shape[1] // 2
    lo = lax.bitcast_convert_type(x[:, :half].astype(BF16).astype(F32), U32)
    hi = lax.bitcast_convert_type(x[:, half:].astype(BF16).astype(F32), U32)
    return hi | (lo >> 16)


def _unpack_bf16_pairs(w):
    lo = lax.bitcast_convert_type(w << 16, F32).astype(BF16)
    hi = lax.bitcast_convert_type(w & jnp.uint32(0xFFFF0000), F32).astype(BF16)
    return jnp.concatenate([lo, hi], axis=1)


def _row_copy(src_ref, dst_ref, src_row, dst_row, sem):
    return pltpu.make_async_copy(src_ref.at[pl.ds(src_row, 1), :], dst_ref.at[pl.ds(dst_row, 1), :], sem)


def _issue_rows(n_tokens, copy_of):
    for t in range(n_tokens):
        for k in range(TOP_K):
            copy_of(t, k).start()


def _drain_rows(src_ref, dst_ref, sem, n_rows):
    def drain(t, carry):
        for _ in range(DRAIN_UNROLL):
            _row_copy(src_ref, dst_ref, 0, 0, sem).wait()
        return carry

    lax.fori_loop(0, n_rows // DRAIN_UNROLL, drain, 0)


def _dispatch_kernel(pstart_ref, cnt_ref, nu_ref, dest_ref, h_ref, xbuf_ref, zero_ref, sem, zsem, *, tb, n_blk):
    @pl.when(pl.program_id(0) == 0)
    def _():
        zero_ref[...] = jnp.zeros(zero_ref.shape, zero_ref.dtype)

        def blk_copy(j):
            return pltpu.make_async_copy(zero_ref, xbuf_ref.at[pl.ds(j * MOE_BM, MOE_BM), :], zsem)

        def blk_start(j, c):
            blk_copy(j).start()
            return c

        def blk_wait(j, c):
            blk_copy(j).wait()
            return c

        lax.fori_loop(nu_ref[0], n_blk, blk_start, 0)
        lax.fori_loop(nu_ref[0], n_blk, blk_wait, 0)

        def per_expert(e, carry):
            cnt = cnt_ref[e]
            first = pstart_ref[e] + cnt
            n_pad = (-cnt) & (MOE_BM - 1)

            def start(r, c):
                _row_copy(zero_ref, xbuf_ref, 0, first + r, zsem).start()
                return c

            def wait(r, c):
                _row_copy(zero_ref, xbuf_ref, 0, 0, zsem).wait()
                return c

            lax.fori_loop(0, n_pad, start, 0)
            lax.fori_loop(0, n_pad, wait, 0)
            return carry

        lax.fori_loop(0, N_EXPERTS, per_expert, 0)

    _issue_rows(tb, lambda t, k: _row_copy(h_ref, xbuf_ref, t, dest_ref[TOP_K * t + k], sem))
    _drain_rows(h_ref, xbuf_ref, sem, TOP_K * tb)


def _dispatch(pstart, counts, n_used, dest, h2w, n_blk):
    t, dw = h2w.shape
    tb = DISP_TB
    grid_spec = pltpu.PrefetchScalarGridSpec(
        num_scalar_prefetch=3,
        grid=(t // tb,),
        in_specs=[
            pl.BlockSpec((TOP_K * tb,), lambda i, ps, cn, nu: (i,), memory_space=pltpu.SMEM),
            pl.BlockSpec((tb, dw), lambda i, ps, cn, nu: (i, 0)),
        ],
        out_specs=pl.BlockSpec(memory_space=pl.ANY),
        scratch_shapes=[
            pltpu.VMEM((MOE_BM, dw), U32),
            pltpu.SemaphoreType.DMA(()),
            pltpu.SemaphoreType.DMA(()),
        ],
    )
    return pl.pallas_call(
        functools.partial(_dispatch_kernel, tb=tb, n_blk=n_blk),
        out_shape=jax.ShapeDtypeStruct((n_blk * MOE_BM, dw), U32),
        grid_spec=grid_spec,
        compiler_params=_params(("arbitrary",)),
        name="dispatch",
    )(pstart, counts, n_used, dest, h2w)


def _expert_kernel(blk_e_ref, n_used_ref, x_ref, w1_ref, w3_ref, w2_ref, y_ref):
    del blk_e_ref
    j = pl.program_id(0)

    @pl.when(j < n_used_ref[0])
    def _():
        x = _unpack_bf16_pairs(x_ref[...])
        a = jnp.dot(x, w1_ref[...], preferred_element_type=F32)
        g = jnp.dot(x, w3_ref[...], preferred_element_type=F32)
        hmid = (a * jax.nn.sigmoid(a)) * g
        y_ref[...] = jnp.dot(hmid.astype(BF16), w2_ref[...], preferred_element_type=F32)

    @pl.when(j >= n_used_ref[0])
    def _():
        y_ref[...] = jnp.zeros(y_ref.shape, F32)


def _experts(blk_e, n_used, xbuf, w1, w3, w2):
    p, dw = xbuf.shape
    d = w1.shape[1]
    bm = MOE_BM
    grid_spec = pltpu.PrefetchScalarGridSpec(
        num_scalar_prefetch=2,
        grid=(p // bm,),
        in_specs=[
            pl.BlockSpec((bm, dw), lambda j, be, nu: (jnp.minimum(j, nu[0] - 1), 0)),
            pl.BlockSpec((None, d, D_FF_EXPERT), lambda j, be, nu: (be[j], 0, 0)),
            pl.BlockSpec((None, d, D_FF_EXPERT), lambda j, be, nu: (be[j], 0, 0)),
            pl.BlockSpec((None, D_FF_EXPERT, d), lambda j, be, nu: (be[j], 0, 0)),
        ],
        out_specs=pl.BlockSpec((bm, d), lambda j, be, nu: (j, 0)),
    )
    return pl.pallas_call(
        _expert_kernel,
        out_shape=jax.ShapeDtypeStruct((p, d), F32),
        grid_spec=grid_spec,
        compiler_params=_params(("arbitrary",)),
        name="experts",
    )(blk_e, n_used, xbuf, w1, w3, w2)


def _final_kernel(dcur_ref, dnxt_ref, x1_ref, route_ref, g2_ref, fg_ref, shf_ref, scf_ref,
                  ybuf_ref, o_ref, rows_ref, sem, *, tc, n_steps):
    i = pl.program_id(0)
    slot = i % 2

    def gather(dest_ref, dst_slot):
        _issue_rows(tc, lambda t, k: pltpu.make_async_copy(
            ybuf_ref.at[pl.ds(dest_ref[TOP_K * t + k], 1), :],
            rows_ref.at[dst_slot, k, pl.ds(t, 1), :], sem.at[dst_slot]))

    @pl.when(i == 0)
    def _():
        gather(dcur_ref, 0)

    @pl.when(i + 1 < n_steps)
    def _():
        gather(dnxt_ref, 1 - slot)

    _drain_rows(ybuf_ref, rows_ref.at[slot, 0], sem.at[slot], TOP_K * tc)

    route = route_ref[...]
    moe = route[:, 4:5] * rows_ref[slot, 0] + route[:, 5:6] * rows_ref[slot, 1]
    x2 = x1_ref[...] + g2_ref[...] * moe
    o_ref[...] = _rms(x2, fg_ref[...]) * (1.0 + scf_ref[...]) + shf_ref[...]


def _final(dest, x1, route, g2, fg, shf, scf, ybuf):
    b, s, d = x1.shape
    tc = FIN_TC
    nt = s // tc
    n_steps = b * nt
    x1f = x1.reshape(b * s, d)
    nxt = lambda i: (jnp.minimum(i + 1, n_steps - 1),)
    vec = lambda i: (i // nt, 0, 0)
    out = pl.pallas_call(
        functools.partial(_final_kernel, tc=tc, n_steps=n_steps),
        out_shape=jax.ShapeDtypeStruct((b * s, d), F32),
        grid=(n_steps,),
        in_specs=[
            pl.BlockSpec((TOP_K * tc,), lambda i: (i,), memory_space=pltpu.SMEM),
            pl.BlockSpec((TOP_K * tc,), nxt, memory_space=pltpu.SMEM),
            pl.BlockSpec((tc, d), lambda i: (i, 0)),
            pl.BlockSpec((tc, ROUTE_LANES), lambda i: (i, 0)),
            pl.BlockSpec((None, 1, d), vec),
            pl.BlockSpec((1, d), lambda i: (0, 0)),
            pl.BlockSpec((None, 1, d), vec),
            pl.BlockSpec((None, 1, d), vec),
            pl.BlockSpec(memory_space=pl.ANY),
        ],
        out_specs=pl.BlockSpec((tc, d), lambda i: (i, 0)),
        scratch_shapes=[
            pltpu.VMEM((2, TOP_K, tc, d), F32),
            pltpu.SemaphoreType.DMA((2,)),
        ],
        compiler_params=_params(("arbitrary",)),
        name="final",
    )(dest, dest, x1f, route, g2, fg, shf, scf, ybuf)
    return out.reshape(b, s, d)


def _t5_buckets():
    nb = NUM_BUCKETS // 2
    max_exact = nb // 2
    koff = jnp.arange(3 * BLOCK) - BLOCK
    rel = koff[None, :] - jnp.arange(BLOCK)[:, None]
    n = jnp.abs(rel)
    large = max_exact + (jnp.log(jnp.maximum(n, 1).astype(F32) / max_exact)
                         / math.log(MAX_DISTANCE / max_exact) * (nb - max_exact)).astype(I32)
    large = jnp.minimum(large, nb - 1)
    return jnp.where(rel > 0, nb, 0) + jnp.where(n < max_exact, n, large)


def _rope_tables(s):
    inv = ROPE_THETA ** (-jnp.arange(ROPE_HALF, dtype=F32) / ROPE_HALF)
    ang = jnp.arange(s).astype(F32)[:, None] * inv[None, :]
    pad = jnp.zeros((s, LANES - QK_ROPE), F32)
    cos = jnp.concatenate([jnp.cos(ang), jnp.cos(ang), pad], axis=1)
    sin = jnp.concatenate([jnp.sin(ang), jnp.sin(ang), pad], axis=1)
    return cos, sin, cos.T, sin.T


def _prep_weights(w_in, w_uq, w_ukv, w_out, router_g, router_g_b, router_e, router_e_b, w1, w3, w2):
    d = w_in.shape[0]
    z64 = jnp.zeros((d, LANES - QK_ROPE), F32)
    kr_w = w_in[:, KR_OFF:KR_OFF + QK_ROPE]
    kr_x1, kr_x2 = kr_w[:, :ROPE_HALF], kr_w[:, ROPE_HALF:]
    win = jnp.concatenate([w_in[:, :KR_OFF], kr_w, z64, -kr_x2, kr_x1, z64, w_in[:, KR_OFF + QK_ROPE:]],
                          axis=1).astype(BF16)
    wq = w_uq.reshape(Q_LORA, N_HEADS_A, QK_NOPE + QK_ROPE)
    nope = wq[:, :, :QK_NOPE].reshape(Q_LORA, N_HEADS_A * QK_NOPE)
    r = wq[:, :, QK_NOPE:]
    zr = jnp.zeros((Q_LORA, N_HEADS_A, LANES - QK_ROPE), F32)
    rot = jnp.concatenate([r, zr], axis=2).reshape(Q_LORA, N_HEADS_A * LANES)
    par = jnp.concatenate([-r[:, :, ROPE_HALF:], r[:, :, :ROPE_HALF], zr], axis=2).reshape(Q_LORA, N_HEADS_A * LANES)
    wqt = jnp.concatenate([nope, rot, par], axis=1).T.astype(BF16)
    wkv = w_ukv.reshape(KV_LORA, N_HEADS_A, QK_NOPE + V_DIM)
    wkn = wkv[:, :, :QK_NOPE].reshape(KV_LORA, N_HEADS_A * QK_NOPE).astype(BF16)
    wvt = wkv[:, :, QK_NOPE:].reshape(KV_LORA, N_HEADS_A * V_DIM).T.astype(BF16)
    pad_r = jnp.zeros((d, ROUTE_LANES - N_GROUPS - N_EXPERTS), F32)
    wr = jnp.concatenate([router_g, router_e, pad_r], axis=1).astype(BF16)
    br = jnp.concatenate([router_g_b, router_e_b, jnp.zeros((ROUTE_LANES - N_GROUPS - N_EXPERTS,), F32)]
                         ).reshape(1, ROUTE_LANES)
    return dict(win=win, wqt=wqt, wkn=wkn, wvt=wvt, wout=w_out.astype(BF16), wr=wr, br=br,
                w1=w1.astype(BF16), w3=w3.astype(BF16), w2=w2.astype(BF16))


def _encoder(x, mod, modf, pw, norm1_g, norm2_g, q_norm_g, kv_norm_g, sink, out_norm_a, out_norm_b,
             bias, final_norm_g):
    b, s, d = x.shape
    t = b * s
    sh1, sc1, g1, sh2, sc2, g2 = [m.reshape(b, 1, d) for m in jnp.split(mod, 6, axis=-1)]
    shf, scf = [m.reshape(b, 1, d) for m in jnp.split(modf, 2, axis=-1)]
    cos, sin, cost, sint = _rope_tables(s)

    qt, kn, kr, vt, qb, kb, vb = _pre(x, sh1, sc1, norm1_g.reshape(1, d), pw["win"], q_norm_g.reshape(1, -1),
                                      kv_norm_g.reshape(1, -1), pw["wqt"], pw["wkn"], pw["wvt"],
                                      cos, sin, cost, sint)
    oa = _mla(qt, kn, kr, vt)
    ob = _swa(sink, qb, kb, vb, bias)
    x1, h2, route, cnt = _post(x, oa, ob, out_norm_a.reshape(1, -1), out_norm_b.reshape(1, -1), pw["wout"],
                               g1, norm2_g.reshape(1, d), sh2, sc2, pw["wr"], pw["br"])

    counts = cnt[0, E_LANE0:E_LANE0 + N_EXPERTS].astype(I32)
    padded = (counts + MOE_BM - 1) // MOE_BM * MOE_BM
    pends = jnp.cumsum(padded)
    pstart = (pends - padded).astype(I32)
    n_blk = (TOP_K * t) // MOE_BM + N_EXPERTS
    blk_first = jnp.arange(n_blk, dtype=I32) * MOE_BM
    blk_e = jnp.minimum(jnp.sum((pends[None, :] <= blk_first[:, None]).astype(I32), axis=1), N_EXPERTS - 1)
    n_used = (pends[-1:] // MOE_BM).astype(I32)
    e_flat = route[:, 0:TOP_K].astype(I32).reshape(-1)
    r_flat = route[:, TOP_K:2 * TOP_K].astype(I32).reshape(-1)

    seg = jnp.sum(jnp.where(e_flat[:, None] == jnp.arange(N_EXPERTS, dtype=I32)[None, :], pstart[None, :], 0), axis=1)
    dest = seg + r_flat

    xbuf = _dispatch(pstart, counts, n_used, dest, h2, n_blk)
    ybuf = _experts(blk_e, n_used, xbuf, pw["w1"], pw["w3"], pw["w2"])
    return _final(dest, x1, route, g2, final_norm_g.reshape(1, d), shf, scf, ybuf)


def kernel(x_prompt, x_sample, c_prompt, c_sample, norm1_g, norm2_g, w_ada, b_ada, w_in, q_norm_g, kv_norm_g,
           w_uq, w_ukv, sink, out_norm_a, out_norm_b, w_out, router_g, router_g_b, router_e, router_e_b,
           w1, w3, w2, rel_bias, final_norm_g, w_ada_f, b_ada_f):
    bp, bs = c_prompt.shape[0], c_sample.shape[0]
    d = c_prompt.shape[1]
    c_rows = jnp.concatenate([c_prompt, c_sample, jnp.zeros((ADA_ROWS - bp - bs, d), F32)], axis=0)
    mod = _ada(c_rows, w_ada[0], b_ada[0])
    modf = _ada(c_rows, w_ada_f, b_ada_f)
    pw = _prep_weights(w_in[0], w_uq[0], w_ukv[0], w_out[0], router_g[0], router_g_b[0], router_e[0],
                       router_e_b[0], w1[0], w3[0], w2[0])
    bias = _bias(rel_bias.astype(F32), _t5_buckets().astype(I32))
    args = (pw, norm1_g[0], norm2_g[0], q_norm_g[0], kv_norm_g[0], sink[0], out_norm_a[0], out_norm_b[0],
            bias, final_norm_g)
    y_prompt = _encoder(x_prompt, mod[:bp], modf[:bp], *args)
    y_sample = _encoder(x_sample, mod[bp:bp + bs], modf[bp:bp + bs], *args)
    return (y_prompt, y_sample)
```
